```python
import functools
import jax, jax.numpy as jnp
from jax import lax
import numpy as np

D_MODEL = 1024
BATCH = 4
SEQ = 4096
DEPTH = 1
DEC_BATCH = 128
DEC_SEQ = 8
PAST_LEN = 8192
PAGE_SIZE = 128

HEAD_DIM_A = 64
D_A = D_MODEL // 2
N_HEADS_A = D_A // HEAD_DIM_A
MOBA_BLOCK = 256
MOBA_TOPK = 3
MOBA_Q_CHUNK = 32
ROPE_DIM = HEAD_DIM_A // 4
ROPE_THETA = 500000.0
D_BV = D_MODEL // 2
N_HEADS_B = 4
DV_B = D_BV // N_HEADS_B
DK_B = DV_B // 2
D_BK = N_HEADS_B * DK_B
GATE_RANK = 16
GATE_NORMALIZER = 16.0
GLA_CHUNK = 64
D_MIX = D_A + D_BV
D_IN = 3 * D_A + 2 * D_BK + 2 * D_BV + GATE_RANK
D_FF = 4 * D_MODEL
EPS = 1e-6
NEG = -1e30

kernel_name = "hymba_moba_gla_decode_step"


def _rmsnorm(x, g):
    xf = x.astype(jnp.float32)
    r = lax.rsqrt(jnp.mean(xf * xf, axis=-1, keepdims=True) + EPS)
    return (xf * r).astype(x.dtype) * g


def _rope_partial(x, pos):
    half = ROPE_DIM // 2
    inv = ROPE_THETA ** (-jnp.arange(half, dtype=jnp.float32) / half)
    ang = pos.astype(jnp.float32)[:, None] * inv[None, :]
    cos = jnp.cos(ang)[None, :, None, :].astype(x.dtype)
    sin = jnp.sin(ang)[None, :, None, :].astype(x.dtype)
    x1, x2, rest = x[..., :half], x[..., half:ROPE_DIM], x[..., ROPE_DIM:]
    return jnp.concatenate([x1 * cos - x2 * sin, x1 * sin + x2 * cos, rest], axis=-1)


def _select_blocks(q, means, n_past, topk):
    s = jnp.einsum('bqhd,bhjd->bqhj', q, means).astype(jnp.float32)
    s = jnp.where(jnp.arange(means.shape[2]) < n_past, s, NEG)
    _, idx = lax.top_k(s, topk)
    valid = jnp.broadcast_to(jnp.arange(topk) < n_past, idx.shape)
    return idx, valid


def _moba_softmax_attend(q, k_own, v_own, own_mask, k_sel=None, v_sel=None, sel_valid=None):
    scale = HEAD_DIM_A ** -0.5
    s_own = jnp.einsum('bqhd,bhld->bqhl', q, k_own).astype(jnp.float32) * scale
    s_own = jnp.where(own_mask[None, :, None, :], s_own, NEG)
    if k_sel is None:
        p = jax.nn.softmax(s_own, axis=-1).astype(q.dtype)
        return jnp.einsum('bqhl,bhld->bqhd', p, v_own)
    s_sel = jnp.einsum('bqhd,bqhknd->bqhkn', q, k_sel).astype(jnp.float32) * scale
    s_sel = jnp.where(sel_valid[..., None], s_sel, NEG)
    B, Q, H, K, N = s_sel.shape
    s = jnp.concatenate([s_sel.reshape(B, Q, H, K * N), s_own], axis=-1)
    p = jax.nn.softmax(s, axis=-1).astype(q.dtype)
    p_sel = p[..., :K * N].reshape(B, Q, H, K, N)
    p_own = p[..., K * N:]
    return (jnp.einsum('bqhkn,bqhknd->bqhd', p_sel, v_sel)
            + jnp.einsum('bqhl,bhld->bqhd', p_own, v_own))


def _moba_prompt(q, k, v):
    B, T, H, d = q.shape
    nb = -(-T // MOBA_BLOCK)
    pad = nb * MOBA_BLOCK - T

    def to_blocks(x):
        xp = jnp.pad(x, ((0, 0), (0, pad), (0, 0), (0, 0)))
        return xp.reshape(B, nb, MOBA_BLOCK, H, d).transpose(0, 3, 1, 2, 4)

    kb, vb = to_blocks(k), to_blocks(v)
    means = jnp.mean(kb, axis=3, dtype=jnp.float32).astype(k.dtype)
    topk = min(MOBA_TOPK, nb)
    qc = min(MOBA_Q_CHUNK, T)
    nc = T // qc
    q_chunks = q.reshape(B, nc, qc, H, d).transpose(1, 0, 2, 3, 4)
    starts = jnp.arange(nc, dtype=jnp.int32) * qc
    b_idx = jnp.arange(B)[:, None, None, None]
    h_idx = jnp.arange(H)[None, None, :, None]

    def one(args):
        qq, start = args
        blk = start // MOBA_BLOCK
        pos = start + jnp.arange(qc)
        idx, valid = _select_blocks(qq, means, blk, topk)
        k_sel = kb[b_idx, h_idx, idx]
        v_sel = vb[b_idx, h_idx, idx]
        k_own = lax.dynamic_index_in_dim(kb, blk, axis=2, keepdims=False)
        v_own = lax.dynamic_index_in_dim(vb, blk, axis=2, keepdims=False)
        own_pos = blk * MOBA_BLOCK + jnp.arange(MOBA_BLOCK)
        own_mask = own_pos[None, :] <= pos[:, None]
        return _moba_softmax_attend(qq, k_own, v_own, own_mask, k_sel, v_sel, valid)

    out = lax.map(one, (q_chunks, starts))
    return out.transpose(1, 0, 2, 3, 4).reshape(B, T, H, d)


def _moba_sample(q, k_new, v_new, cache_k, cache_v, page_table):
    DB, T, H, d = q.shape
    page = cache_k.shape[2]
    n_pages = page_table.shape[1]
    past = n_pages * page
    ppb = MOBA_BLOCK // page
    n_full = past // MOBA_BLOCK
    n_own_pages = n_pages - n_full * ppb
    kn = k_new.transpose(0, 2, 1, 3)
    vn = v_new.transpose(0, 2, 1, 3)
    causal_new = jnp.arange(T)[None, :] <= jnp.arange(T)[:, None]
    if n_own_pages > 0:
        own_pt = page_table[:, n_full * ppb:]
        k_op = cache_k[own_pt].transpose(0, 2, 1, 3, 4).reshape(DB, H, n_own_pages * page, d)
        v_op = cache_v[own_pt].transpose(0, 2, 1, 3, 4).reshape(DB, H, n_own_pages * page, d)
        k_own = jnp.concatenate([k_op, kn], axis=2)
        v_own = jnp.concatenate([v_op, vn], axis=2)
        own_mask = jnp.concatenate([jnp.ones((T, n_own_pages * page), bool), causal_new], axis=1)
    else:
        k_own, v_own, own_mask = kn, vn, causal_new
    if n_full == 0:
        return _moba_softmax_attend(q, k_own, v_own, own_mask)
    page_means = jnp.mean(cache_k, axis=2, dtype=jnp.float32)
    means = page_means[page_table[:, :n_full * ppb]]
    means = means.reshape(DB, n_full, ppb, H, d).mean(axis=2)
    means = means.transpose(0, 2, 1, 3).astype(q.dtype)
    topk = min(MOBA_TOPK, n_full)
    idx, _ = _select_blocks(q, means, n_full, topk)
    b_idx = jnp.arange(DB)[:, None, None, None, None]
    logical = idx[..., None] * ppb + jnp.arange(ppb)
    phys = page_table[b_idx, logical]
    h_idx = jnp.arange(H)[None, :, None, None]
    valid1 = jnp.ones((DB, 1, H, topk), bool)

    def one(args):
        qq, ph, om = args
        k_sel = cache_k[ph, h_idx].reshape(DB, H, topk, ppb * page, d)[:, None]
        v_sel = cache_v[ph, h_idx].reshape(DB, H, topk, ppb * page, d)[:, None]
        return _moba_softmax_attend(qq, k_own, v_own, om[None], k_sel, v_sel, valid1)

    out = lax.map(one, (q.transpose(1, 0, 2, 3)[:, :, None], phys.transpose(1, 0, 2, 3, 4), own_mask))
    return out[:, :, 0].transpose(1, 0, 2, 3)


def _gla(q, k, v, gk, S0, chunk):
    B, T, H, dk = q.shape
    dv = v.shape[-1]
    nc = T // chunk

    def to_chunks(x):
        return x.reshape(B, nc, chunk, H, x.shape[-1]).transpose(1, 0, 3, 2, 4)

    causal = jnp.arange(chunk)[:, None] >= jnp.arange(chunk)[None, :]

    def step(S, xs):
        qq, kk, vv, gg = xs
        qf, kf, vf = qq.astype(jnp.float32), kk.astype(jnp.float32), vv.astype(jnp.float32)
        cum = jnp.cumsum(gg.astype(jnp.float32), axis=2)
        inter = jnp.einsum('bhtd,bhde->bhte', qf * jnp.exp(cum), S)
        diff = jnp.where(causal[:, :, None], cum[:, :, :, None, :] - cum[:, :, None, :, :], NEG)
        A = jnp.einsum('bhtd,bhsd,bhtsd->bhts', qf, kf, jnp.exp(diff))
        o = inter + jnp.einsum('bhts,bhse->bhte', A, vf)
        tot = cum[:, :, -1:, :]
        S_new = (S * jnp.exp(tot[:, :, 0, :])[..., None]
                 + jnp.einsum('bhsd,bhse->bhde', kf * jnp.exp(tot - cum), vf))
        return S_new, o

    S, o = lax.scan(step, S0.astype(jnp.float32),
                    (to_chunks(q), to_chunks(k), to_chunks(v), to_chunks(gk)))
    o = o.transpose(1, 0, 3, 2, 4).reshape(B, T, H, dv)
    return o.astype(v.dtype), S.astype(S0.dtype)


def _block(x, pos, S0, gla_chunk, moba_fn, norm_mix, w_in, w_gk2, b_gk, gla_norm, w_o,
           norm_mlp, w_up, w_down):
    B, T, _ = x.shape
    h = _rmsnorm(x, norm_mix)
    z = h @ w_in
    sizes = [D_A, D_A, D_A, D_BK, D_BK, D_BV, D_BV, GATE_RANK]
    splits = [int(s) for s in np.cumsum(sizes)[:-1]]
    qa, ka, va, qb, kb, vb, gb, lr = jnp.split(z, splits, axis=-1)
    qa = _rope_partial(qa.reshape(B, T, N_HEADS_A, HEAD_DIM_A), pos)
    ka = _rope_partial(ka.reshape(B, T, N_HEADS_A, HEAD_DIM_A), pos)
    va = va.reshape(B, T, N_HEADS_A, HEAD_DIM_A)
    oa = moba_fn(qa, ka, va)
    gk = jax.nn.log_sigmoid((lr @ w_gk2 + b_gk).astype(jnp.float32)) / GATE_NORMALIZER
    ob, S_new = _gla(qb.reshape(B, T, N_HEADS_B, DK_B) * (DK_B ** -0.5),
                     kb.reshape(B, T, N_HEADS_B, DK_B),
                     vb.reshape(B, T, N_HEADS_B, DV_B),
                     gk.reshape(B, T, N_HEADS_B, DK_B), S0, gla_chunk)
    ob = _rmsnorm(ob, gla_norm.reshape(N_HEADS_B, DV_B)) * jax.nn.silu(gb.reshape(B, T, N_HEADS_B, DV_B))
    mix = jnp.concatenate([oa.reshape(B, T, D_A), ob.reshape(B, T, D_BV)], axis=-1)
    x = x + mix @ w_o
    u = jax.nn.relu(_rmsnorm(x, norm_mlp) @ w_up)
    x = x + (u * u) @ w_down
    return x, ka.transpose(0, 2, 1, 3), va.transpose(0, 2, 1, 3), S_new


def setup_inputs(seed: int = 0) -> dict:
    key = jax.random.key(seed)
    ks = jax.random.split(key, 16)
    n_pages = PAST_LEN // PAGE_SIZE
    n_pool = (DEC_BATCH * n_pages * 5) // 4
    f32 = jnp.float32
    x_prompt = jax.random.normal(ks[0], (BATCH, SEQ, D_MODEL), f32)
    x_sample = jax.random.normal(ks[1], (DEC_BATCH, DEC_SEQ, D_MODEL), f32)
    cache_k = jax.random.normal(ks[2], (DEPTH, n_pool, N_HEADS_A, PAGE_SIZE, HEAD_DIM_A), f32)
    cache_v = jax.random.normal(ks[3], (DEPTH, n_pool, N_HEADS_A, PAGE_SIZE, HEAD_DIM_A), f32)
    state_gla = jax.random.normal(ks[4], (DEPTH, DEC_BATCH, N_HEADS_B, DK_B, DV_B), f32)
    page_table = jax.random.permutation(ks[5], n_pool)[:DEC_BATCH * n_pages]
    page_table = page_table.reshape(DEC_BATCH, n_pages).astype(jnp.int32)
    norm_mix = 1.0 + 0.01 * jax.random.normal(ks[6], (DEPTH, D_MODEL), f32)
    w_in = jax.random.normal(ks[7], (DEPTH, D_MODEL, D_IN), f32) * D_MODEL ** -0.5
    w_gk2 = jax.random.normal(ks[8], (DEPTH, GATE_RANK, D_BK), f32) * GATE_RANK ** -0.5
    b_gk = 0.1 * jax.random.normal(ks[9], (DEPTH, D_BK), f32)
    gla_norm = 1.0 + 0.01 * jax.random.normal(ks[10], (DEPTH, D_BV), f32)
    w_o = jax.random.normal(ks[11], (DEPTH, D_MIX, D_MODEL), f32) * D_MIX ** -0.5
    norm_mlp = 1.0 + 0.01 * jax.random.normal(ks[12], (DEPTH, D_MODEL), f32)
    w_up = jax.random.normal(ks[13], (DEPTH, D_MODEL, D_FF), f32) * D_MODEL ** -0.5
    w_down = jax.random.normal(ks[14], (DEPTH, D_FF, D_MODEL), f32) * D_FF ** -0.5
    norm_final = 1.0 + 0.01 * jax.random.normal(ks[15], (D_MODEL,), f32)
    return {"x_prompt": x_prompt, "x_sample": x_sample, "cache_k": cache_k, "cache_v": cache_v,
            "state_gla": state_gla, "page_table": page_table, "norm_mix": norm_mix, "w_in": w_in,
            "w_gk2": w_gk2, "b_gk": b_gk, "gla_norm": gla_norm, "w_o": w_o, "norm_mlp": norm_mlp,
            "w_up": w_up, "w_down": w_down, "norm_final": norm_final}


def reference(x_prompt, x_sample, cache_k, cache_v, state_gla, page_table, norm_mix, w_in, w_gk2,
              b_gk, gla_norm, w_o, norm_mlp, w_up, w_down, norm_final):
    B, T_p, _ = x_prompt.shape
    T_s = x_sample.shape[1]
    past = page_table.shape[1] * cache_k.shape[3]
    pos_p = jnp.arange(T_p, dtype=jnp.int32)
    pos_s = past + jnp.arange(T_s, dtype=jnp.int32)
    xp, xs = x_prompt, x_sample
    kp_l, vp_l, sp_l, ks_l, vs_l, ss_l = [], [], [], [], [], []
    for l in range(DEPTH):
        p = (norm_mix[l], w_in[l], w_gk2[l], b_gk[l], gla_norm[l], w_o[l], norm_mlp[l], w_up[l], w_down[l])
        S0_p = jnp.zeros((B, N_HEADS_B, DK_B, DV_B), state_gla.dtype)
        xp, kp, vp, sp = _block(xp, pos_p, S0_p, min(GLA_CHUNK, T_p), _moba_prompt, *p)
        moba_s = functools.partial(_moba_sample, cache_k=cache_k[l], cache_v=cache_v[l],
                                   page_table=page_table)
        xs, ks_, vs_, ss = _block(xs, pos_s, state_gla[l], T_s, moba_s, *p)
        kp_l.append(kp); vp_l.append(vp); sp_l.append(sp)
        ks_l.append(ks_); vs_l.append(vs_); ss_l.append(ss)
    y_prompt = _rmsnorm(xp, norm_final)
    y_sample = _rmsnorm(xs, norm_final)
    return (y_prompt, y_sample, jnp.stack(kp_l), jnp.stack(vp_l), jnp.stack(sp_l),
            jnp.stack(ks_l), jnp.stack(vs_l), jnp.stack(ss_l))
```

```python
import functools

import jax
import jax.numpy as jnp
import numpy as np
from jax import lax
from jax.experimental import pallas as pl
from jax.experimental.pallas import tpu as pltpu

D_MODEL = 1024
HEAD_DIM_A = 64
D_A = D_MODEL // 2
N_HEADS_A = D_A // HEAD_DIM_A
MOBA_BLOCK = 256
MOBA_TOPK = 3
ROPE_DIM = HEAD_DIM_A // 4
ROPE_THETA = 500000.0
D_BV = D_MODEL // 2
N_HEADS_B = 4
DV_B = D_BV // N_HEADS_B
DK_B = DV_B // 2
D_BK = N_HEADS_B * DK_B
GATE_RANK = 16
GATE_NORMALIZER = 16.0
D_IN = 3 * D_A + 2 * D_BK + 2 * D_BV + GATE_RANK
D_FF = 4 * D_MODEL
EPS = 1e-6
NEG = -1e30

LANES = 128
D_IN_PAD = D_IN - GATE_RANK + LANES
_OFF_QA, _OFF_KA, _OFF_VA = 0, D_A, 2 * D_A
_OFF_QB = 3 * D_A
_OFF_KB = _OFF_QB + D_BK
_OFF_VB = _OFF_KB + D_BK
_OFF_GB = _OFF_VB + D_BV
_OFF_LR = _OFF_GB + D_BV

TOKEN_TILE = 256
GLA_CHUNK_PROMPT = 16
GLA_TBLOCK = 256
FF_TILE = 1024
VMEM_LIMIT = 48 * 1024 * 1024

_HI = lax.Precision.HIGHEST
_NT = (((1,), (1,)), ((), ()))
_TN = (((0,), (0,)), ((), ()))


def _bf(x):
    return x.astype(jnp.bfloat16)


def _rms(x, g):
    r = lax.rsqrt(jnp.mean(x * x, axis=-1, keepdims=True) + EPS)
    return (x * r) * g


def _rope(z, cos, sin_lo, sin_hi):
    n = z.shape[1]
    cos = jnp.concatenate([cos] * (n // LANES), axis=1)
    sin_lo = jnp.concatenate([sin_lo] * (n // LANES), axis=1)
    sin_hi = jnp.concatenate([sin_hi] * (n // LANES), axis=1)
    half = ROPE_DIM // 2
    up = pltpu.roll(z, n - half, 1)
    dn = pltpu.roll(z, half, 1)
    return z * cos + up * sin_lo + dn * sin_hi


def _in_proj_kernel(sample, x_ref, g_ref, w_ref, wg_ref, bg_ref, cos_ref, slo_ref, shi_ref, *outs):
    tm = x_ref.shape[0]
    hb = _bf(_rms(x_ref[...], g_ref[...]))

    def proj(off, width):
        return jnp.dot(hb, w_ref[:, off:off + width], preferred_element_type=jnp.float32)

    cos, slo, shi = cos_ref[...], slo_ref[...], shi_ref[...]
    q = _rope(proj(_OFF_QA, D_A), cos, slo, shi)
    k = _rope(proj(_OFF_KA, D_A), cos, slo, shi)
    v = proj(_OFF_VA, D_A)
    if sample:
        (q_hm, k_hm, v_hm, qb_o, kb_o, vb_o, gb_o, gk_o) = outs
        nseq = q_hm.shape[0]
        for h in range(N_HEADS_A):
            sl = slice(h * HEAD_DIM_A, (h + 1) * HEAD_DIM_A)
            q_hm[:, h] = q[:, sl].reshape(nseq, tm // nseq, HEAD_DIM_A)
            k_hm[:, h] = k[:, sl].reshape(nseq, tm // nseq, HEAD_DIM_A)
            v_hm[:, h] = v[:, sl].reshape(nseq, tm // nseq, HEAD_DIM_A)
    else:
        (qs_o, qf_o, kbf_o, vt_o, k_hm, v_hm, mean_o, qb_o, kb_o, vb_o, gb_o, gk_o) = outs
        qs_o[...] = _bf(q * (HEAD_DIM_A ** -0.5))
        qf_o[...] = q
        kbf_o[...] = _bf(k)
        vt_o[0, 0] = _bf(v.T)
        mean_o[0, 0] = jnp.sum(k, axis=0, keepdims=True) * (1.0 / tm)
        for h in range(N_HEADS_A):
            sl = slice(h * HEAD_DIM_A, (h + 1) * HEAD_DIM_A)
            k_hm[0, h] = k[:, sl]
            v_hm[0, h] = v[:, sl]
    qb_o[...] = proj(_OFF_QB, D_BK) * (DK_B ** -0.5)
    kb_o[...] = proj(_OFF_KB, D_BK)
    vb_o[...] = proj(_OFF_VB, D_BV)
    gb_o[...] = proj(_OFF_GB, D_BV)
    lr = _bf(proj(_OFF_LR, LANES))
    zg = jnp.dot(lr, wg_ref[...], preferred_element_type=jnp.float32) + bg_ref[...]
    log_sig = jnp.minimum(zg, 0.0) - jnp.log1p(jnp.exp(-jnp.abs(zg)))
    gk_o[...] = log_sig * (1.0 / GATE_NORMALIZER)


def _in_proj(x2d, norm_mix, w_in_p, w_gk2_p, b_gk, tables, *, sample, nseq, seq_len):
    n = x2d.shape[0]
    tm = TOKEN_TILE
    assert n % tm == 0
    nt = n // tm
    cos, slo, shi = tables
    f32 = jnp.float32
    row = lambda w: pl.BlockSpec((tm, w), lambda i: (i, 0))
    full = lambda a: pl.BlockSpec(a.shape, lambda i: (0,) * a.ndim)
    gla_shapes = [jax.ShapeDtypeStruct((n, D_BK), f32), jax.ShapeDtypeStruct((n, D_BK), f32),
                  jax.ShapeDtypeStruct((n, D_BV), f32), jax.ShapeDtypeStruct((n, D_BV), f32),
                  jax.ShapeDtypeStruct((n, D_BK), f32)]
    gla_specs = [row(D_BK), row(D_BK), row(D_BV), row(D_BV), row(D_BK)]
    if sample:
        assert tm % seq_len == 0
        spt = tm // seq_len
        hm = jax.ShapeDtypeStruct((nseq, N_HEADS_A, seq_len, HEAD_DIM_A), f32)
        hm_spec = pl.BlockSpec((spt, N_HEADS_A, seq_len, HEAD_DIM_A), lambda i: (i, 0, 0, 0))
        out_shape = [hm, hm, hm] + gla_shapes
        out_specs = [hm_spec, hm_spec, hm_spec] + gla_specs
        tab_spec = row(LANES)
    else:
        assert seq_len % tm == 0 and tm == MOBA_BLOCK
        tps = seq_len // tm
        nb = tps
        hm = jax.ShapeDtypeStruct((nseq, N_HEADS_A, seq_len, HEAD_DIM_A), f32)
        hm_spec = pl.BlockSpec((1, N_HEADS_A, tm, HEAD_DIM_A), lambda i: (i // tps, 0, i % tps, 0))
        out_shape = [jax.ShapeDtypeStruct((n, D_A), jnp.bfloat16),
                     jax.ShapeDtypeStruct((n, D_A), f32),
                     jax.ShapeDtypeStruct((n, D_A), jnp.bfloat16),
                     jax.ShapeDtypeStruct((nseq, nb, D_A, tm), jnp.bfloat16),
                     hm, hm,
                     jax.ShapeDtypeStruct((nseq, nb, 1, D_A), f32)] + gla_shapes
        out_specs = [row(D_A), row(D_A), row(D_A),
                     pl.BlockSpec((1, 1, D_A, tm), lambda i: (i // tps, i % tps, 0, 0)),
                     hm_spec, hm_spec,
                     pl.BlockSpec((1, 1, 1, D_A), lambda i: (i // tps, i % tps, 0, 0))] + gla_specs
        tab_spec = pl.BlockSpec((tm, LANES), lambda i: (i % tps, 0))
    return pl.pallas_call(
        functools.partial(_in_proj_kernel, sample),
        grid=(nt,),
        in_specs=[row(D_MODEL), full(norm_mix), full(w_in_p), full(w_gk2_p), full(b_gk),
                  tab_spec, tab_spec, tab_spec],
        out_specs=out_specs,
        out_shape=out_shape,
        compiler_params=pltpu.CompilerParams(dimension_semantics=("arbitrary",),
                                             vmem_limit_bytes=VMEM_LIMIT),
        name="in_proj_sample" if sample else "in_proj_prompt",
    )(x2d, norm_mix, w_in_p, w_gk2_p, b_gk, cos, slo, shi)


def _rope_tables(pos):
    half = ROPE_DIM // 2
    inv = ROPE_THETA ** (-jnp.arange(half, dtype=jnp.float32) / half)
    ang = pos.astype(jnp.float32)[:, None] * inv[None, :]
    c, s = jnp.cos(ang), jnp.sin(ang)
    n = pos.shape[0]
    ones = jnp.ones((n, HEAD_DIM_A - ROPE_DIM), jnp.float32)
    zeros = jnp.zeros((n, HEAD_DIM_A - ROPE_DIM), jnp.float32)
    zh = jnp.zeros((n, half), jnp.float32)
    cos = jnp.concatenate([c, c, ones], axis=1)
    sin_lo = jnp.concatenate([-s, zh, zeros], axis=1)
    sin_hi = jnp.concatenate([zh, s, zeros], axis=1)
    rep = LANES // HEAD_DIM_A
    return tuple(jnp.concatenate([t] * rep, axis=1) for t in (cos, sin_lo, sin_hi))


def _rank_desc(s, blk, nb):
    rank = jnp.zeros(s.shape, jnp.float32)
    for jp in range(nb):
        row = s[jp:jp + 1, :]
        before = (row > s) | ((row == s) & (blk > jp))
        rank = rank + jnp.where(before, 1.0, 0.0)
    return rank


def _moba_kernel(qs_ref, qf_ref, k_ref, vt_ref, mean_ref, o_ref, sel_ref, m_ref, l_ref, acc_ref):
    i = pl.program_id(2)
    bq = qs_ref.shape[1]
    nb = mean_ref.shape[1]
    hd = HEAD_DIM_A
    q2 = qs_ref[0]
    lane = lax.broadcasted_iota(jnp.int32, q2.shape, 1)
    halves = (lane < hd, lane >= hd)
    qm = [jnp.where(hm, q2, jnp.zeros_like(q2)) for hm in halves]

    qf = qf_ref[0]
    means = mean_ref[0]
    blk = lax.broadcasted_iota(jnp.int32, (nb, bq), 0)
    for h in range(2):
        qfh = jnp.where(halves[h], qf, 0.0)
        s = lax.dot_general(means, qfh, _NT, precision=_HI, preferred_element_type=jnp.float32)
        s = jnp.where(blk < i, s, NEG)
        rank = _rank_desc(s, blk, nb)
        sel_ref[h] = jnp.where((rank < MOBA_TOPK) & (blk < i), 1.0, 0.0)

    kd = k_ref[0, pl.ds(pl.multiple_of(i * bq, bq), bq), :]
    kidx = lax.broadcasted_iota(jnp.int32, (bq, bq), 0)
    qidx = lax.broadcasted_iota(jnp.int32, (bq, bq), 1)
    for h in range(2):
        s = lax.dot_general(kd, qm[h], _NT, preferred_element_type=jnp.float32)
        s = jnp.where(kidx <= qidx, s, NEG)
        m = jnp.max(s, axis=0, keepdims=True)
        p = jnp.exp(s - m)
        m_ref[h] = m
        l_ref[h] = jnp.sum(p, axis=0, keepdims=True)
        acc_ref[h] = jnp.dot(vt_ref[0, i, h * hd:(h + 1) * hd, :], _bf(p),
                             preferred_element_type=jnp.float32)

    def past(j, carry):
        kj = k_ref[0, pl.ds(pl.multiple_of(j * bq, bq), bq), :]
        for h in range(2):
            s = lax.dot_general(kj, qm[h], _NT, preferred_element_type=jnp.float32)
            s = jnp.where(sel_ref[h, pl.ds(j, 1), :] > 0.0, s, NEG)
            m_old = m_ref[h]
            m_new = jnp.maximum(m_old, jnp.max(s, axis=0, keepdims=True))
            alpha = jnp.exp(m_old - m_new)
            p = jnp.exp(s - m_new)
            m_ref[h] = m_new
            l_ref[h] = alpha * l_ref[h] + jnp.sum(p, axis=0, keepdims=True)
            acc_ref[h] = alpha * acc_ref[h] + jnp.dot(vt_ref[0, j, h * hd:(h + 1) * hd, :], _bf(p),
                                                      preferred_element_type=jnp.float32)
        return carry

    lax.fori_loop(0, i, past, 0)
    ot = jnp.concatenate([acc_ref[0] / l_ref[0], acc_ref[1] / l_ref[1]], axis=0)
    o_ref[0] = ot.T


def _moba_prompt(qs, qf, kbf, vt, means):
    b, t, _ = qs.shape
    nb = t // MOBA_BLOCK
    bq = MOBA_BLOCK
    npair = D_A // LANES
    return pl.pallas_call(
        _moba_kernel,
        grid=(b, npair, nb),
        in_specs=[pl.BlockSpec((1, bq, LANES), lambda b_, p, i: (b_, i, p)),
                  pl.BlockSpec((1, bq, LANES), lambda b_, p, i: (b_, i, p)),
                  pl.BlockSpec((1, t, LANES), lambda b_, p, i: (b_, 0, p)),
                  pl.BlockSpec((1, nb, LANES, bq), lambda b_, p, i: (b_, 0, p, 0)),
                  pl.BlockSpec((1, nb, LANES), lambda b_, p, i: (b_, 0, p))],
        out_specs=pl.BlockSpec((1, bq, LANES), lambda b_, p, i: (b_, i, p)),
        out_shape=jax.ShapeDtypeStruct((b, t, D_A), jnp.float32),
        scratch_shapes=[pltpu.VMEM((2, nb, bq), jnp.float32),
                        pltpu.VMEM((2, 1, bq), jnp.float32),
                        pltpu.VMEM((2, 1, bq), jnp.float32),
                        pltpu.VMEM((2, HEAD_DIM_A, bq), jnp.float32)],
        compiler_params=pltpu.CompilerParams(
            dimension_semantics=("arbitrary", "arbitrary", "arbitrary"),
            vmem_limit_bytes=VMEM_LIMIT),
        name="moba_prompt",
    )(qs, qf, kbf, vt, means)


def _page_copy(cache_ref, layer, page, head, dst_ref, sem):
    return pltpu.make_async_copy(cache_ref.at[layer, page, head], dst_ref, sem)


def _sel_kernel(layer, nseq, nh, n_pages, pt_ref, q_ref, ck_ref, ids_ref, kbuf, sem):
    b, h = pl.program_id(0), pl.program_id(1)
    page = ck_ref.shape[3]
    nblk = kbuf.shape[1]
    ppb = kbuf.shape[2] // page
    step = b * nh + h
    slot = step % 2

    def dst(sl, pg):
        return kbuf.at[sl, pg // ppb, pl.ds((pg % ppb) * page, page), :]

    def issue(bb, hh, sl):
        def body(pg, c):
            _page_copy(ck_ref, layer, pt_ref[bb * n_pages + pg], hh, dst(sl, pg), sem.at[sl]).start()
            return c
        lax.fori_loop(0, n_pages, body, 0)

    @pl.when(step == 0)
    def _():
        issue(b, h, slot)

    nxt = step + 1

    @pl.when(nxt < nseq * nh)
    def _():
        issue(nxt // nh, nxt % nh, 1 - slot)

    def wait_body(pg, c):
        _page_copy(ck_ref, layer, 0, 0, dst(slot, pg), sem.at[slot]).wait()
        return c
    lax.fori_loop(0, n_pages, wait_body, 0)

    means = jnp.sum(kbuf[slot], axis=1) * (1.0 / (ppb * page))
    q = q_ref[0, 0]
    ts = q.shape[0]
    qpad = jnp.concatenate([q, jnp.zeros((LANES - ts, q.shape[1]), q.dtype)], axis=0)
    s = lax.dot_general(means, qpad, _NT, precision=_HI, preferred_element_type=jnp.float32)
    blk = lax.broadcasted_iota(jnp.int32, (nblk, LANES), 0)
    rank = _rank_desc(s, blk, nblk)
    rows = [jnp.sum(jnp.where(rank == float(r), blk, 0), axis=0, keepdims=True)
            for r in range(MOBA_TOPK)]
    rows.append(jnp.zeros((8 - MOBA_TOPK, LANES), jnp.int32))
    ids_ref[0, 0] = jnp.concatenate(rows, axis=0)


def _sample_select(layer, page_table, q_hm, cache_k):
    db, nh, ts, hd = q_hm.shape
    n_pages = page_table.shape[1]
    page = cache_k.shape[3]
    assert MOBA_BLOCK % page == 0 and n_pages % (MOBA_BLOCK // page) == 0
    ppb = MOBA_BLOCK // page
    nblk = n_pages // ppb
    grid_spec = pltpu.PrefetchScalarGridSpec(
        num_scalar_prefetch=1,
        grid=(db, nh),
        in_specs=[pl.BlockSpec((1, 1, ts, hd), lambda b, h, pt: (b, h, 0, 0)),
                  pl.BlockSpec(memory_space=pl.ANY)],
        out_specs=pl.BlockSpec((1, 1, 8, LANES), lambda b, h, pt: (b, h, 0, 0)),
        scratch_shapes=[pltpu.VMEM((2, nblk, MOBA_BLOCK, hd), jnp.float32),
                        pltpu.SemaphoreType.DMA((2,))])
    return pl.pallas_call(
        functools.partial(_sel_kernel, layer, db, nh, n_pages),
        grid_spec=grid_spec,
        out_shape=jax.ShapeDtypeStruct((db, nh, 8, LANES), jnp.int32),
        compiler_params=pltpu.CompilerParams(dimension_semantics=("arbitrary", "arbitrary"),
                                             vmem_limit_bytes=VMEM_LIMIT),
        name="sample_select",
    )(page_table.reshape(-1), q_hm, cache_k)


def _att_kernel(layer, nseq, nh, phys_ref, q_ref, kn_ref, vn_ref, ck_ref, cv_ref, o_ref, kbuf, vbuf,
                sem):
    b, h = pl.program_id(0), pl.program_id(1)
    npg = kbuf.shape[1]
    page = kbuf.shape[2]
    hd = kbuf.shape[3]
    step = b * nh + h
    slot = step % 2

    def issue(st, hh, sl):
        def body(g, c):
            phys = phys_ref[st * npg + g]
            _page_copy(ck_ref, layer, phys, hh, kbuf.at[sl, g], sem.at[0, sl]).start()
            _page_copy(cv_ref, layer, phys, hh, vbuf.at[sl, g], sem.at[1, sl]).start()
            return c
        lax.fori_loop(0, npg, body, 0)

    @pl.when(step == 0)
    def _():
        issue(step, h, slot)

    nxt = step + 1

    @pl.when(nxt < nseq * nh)
    def _():
        issue(nxt, nxt % nh, 1 - slot)

    def wait_body(g, c):
        _page_copy(ck_ref, layer, 0, 0, kbuf.at[slot, g], sem.at[0, slot]).wait()
        _page_copy(cv_ref, layer, 0, 0, vbuf.at[slot, g], sem.at[1, slot]).wait()
        return c
    lax.fori_loop(0, npg, wait_body, 0)

    q = q_ref[0, 0] * (hd ** -0.5)
    ts = q.shape[0]
    per_q = npg // ts * page
    qb = _bf(q)
    kg = _bf(kbuf[slot].reshape(npg * page, hd))
    vg = _bf(vbuf[slot].reshape(npg * page, hd))
    s = lax.dot_general(qb, kg, _NT, preferred_element_type=jnp.float32)
    off = (lax.broadcasted_iota(jnp.int32, s.shape, 1)
           - per_q * lax.broadcasted_iota(jnp.int32, s.shape, 0))
    s = jnp.where((off >= 0) & (off < per_q), s, NEG)
    pad = jnp.zeros((LANES - ts, hd), jnp.float32)
    kn = _bf(jnp.concatenate([kn_ref[0, 0], pad], axis=0))
    vn = _bf(jnp.concatenate([vn_ref[0, 0], pad], axis=0))
    so = lax.dot_general(qb, kn, _NT, preferred_element_type=jnp.float32)
    col = lax.broadcasted_iota(jnp.int32, so.shape, 1)
    so = jnp.where(col <= lax.broadcasted_iota(jnp.int32, so.shape, 0), so, NEG)
    m = jnp.maximum(jnp.max(s, axis=1, keepdims=True), jnp.max(so, axis=1, keepdims=True))
    p = jnp.exp(s - m)
    po = jnp.exp(so - m)
    l = jnp.sum(p, axis=1, keepdims=True) + jnp.sum(po, axis=1, keepdims=True)
    o = (jnp.dot(_bf(p), vg, preferred_element_type=jnp.float32)
         + jnp.dot(_bf(po), vn, preferred_element_type=jnp.float32))
    o_ref[0, 0] = o / l


def _sample_attend(layer, phys, q_hm, k_hm, v_hm, cache_k, cache_v, npg):
    db, nh, ts, hd = q_hm.shape
    page = cache_k.shape[3]
    blk = pl.BlockSpec((1, 1, ts, hd), lambda b, h, ph: (b, h, 0, 0))
    grid_spec = pltpu.PrefetchScalarGridSpec(
        num_scalar_prefetch=1,
        grid=(db, nh),
        in_specs=[blk, blk, blk, pl.BlockSpec(memory_space=pl.ANY), pl.BlockSpec(memory_space=pl.ANY)],
        out_specs=blk,
        scratch_shapes=[pltpu.VMEM((2, npg, page, hd), jnp.float32),
                        pltpu.VMEM((2, npg, page, hd), jnp.float32),
                        pltpu.SemaphoreType.DMA((2, 2))])
    return pl.pallas_call(
        functools.partial(_att_kernel, layer, db, nh),
        grid_spec=grid_spec,
        out_shape=jax.ShapeDtypeStruct((db, nh, ts, hd), jnp.float32),
        compiler_params=pltpu.CompilerParams(dimension_semantics=("arbitrary", "arbitrary"),
                                             vmem_limit_bytes=VMEM_LIMIT),
        name="sample_attend",
    )(phys, q_hm, k_hm, v_hm, cache_k, cache_v)


def _gla_kernel(chunk, q_ref, k_ref, g_ref, v_ref, s0_ref, seg_ref, o_ref, sout_ref, s_ref):
    tb = pl.program_id(1)
    c = chunk
    nchunk = q_ref.shape[1] // c
    dk, dv = DK_B, DV_B

    @pl.when(tb == 0)
    def _():
        s_ref[...] = s0_ref[0]

    rows = lax.broadcasted_iota(jnp.int32, (c, c), 0)
    cols = lax.broadcasted_iota(jnp.int32, (c, c), 1)
    ltri = jnp.where(rows >= cols, 1.0, 0.0)
    ones = jnp.ones((c, dv), jnp.float32)
    trow = lax.broadcasted_iota(jnp.int32, (c, D_BK), 0)

    def body(ci, carry):
        t0 = pl.multiple_of(ci * c, c)
        q = q_ref[0, pl.ds(t0, c), :]
        k = k_ref[0, pl.ds(t0, c), :]
        g = g_ref[0, pl.ds(t0, c), :]
        v = v_ref[0, pl.ds(t0, c), :]
        cum = jnp.dot(ltri, g, precision=_HI, preferred_element_type=jnp.float32)
        tot = cum[c - 1:c, :]
        qe = _bf(q * jnp.exp(cum))
        kd = _bf(k * jnp.exp(tot - cum))
        parts = []
        for s in range(c):
            e = jnp.exp(jnp.where(trow >= s, cum - cum[s:s + 1, :], NEG))
            parts.append(q * k[s:s + 1, :] * e)
        pmat = _bf(jnp.concatenate(parts, axis=0))
        r = jnp.dot(pmat, seg_ref[...], preferred_element_type=jnp.float32)
        o = r[0:c] * v[0:1, :]
        for s in range(1, c):
            o = o + r[s * c:(s + 1) * c] * v[s:s + 1, :]
        st = s_ref[...]
        inter = [jnp.dot(qe[:, h * dk:(h + 1) * dk], _bf(st[h * dk:(h + 1) * dk, :]),
                         preferred_element_type=jnp.float32) for h in range(N_HEADS_B)]
        o_ref[0, pl.ds(t0, c), :] = o + jnp.concatenate(inter, axis=1)
        decay = jnp.exp(lax.dot_general(g, ones, _TN, precision=_HI,
                                        preferred_element_type=jnp.float32))
        vb = _bf(v)
        upd = [lax.dot_general(kd[:, h * dk:(h + 1) * dk], vb[:, h * dv:(h + 1) * dv], _TN,
                               preferred_element_type=jnp.float32) for h in range(N_HEADS_B)]
        s_ref[...] = st * decay + jnp.concatenate(upd, axis=0)
        return carry

    lax.fori_loop(0, nchunk, body, 0)

    @pl.when(tb == pl.num_programs(1) - 1)
    def _():
        sout_ref[0] = s_ref[...]


def _gla(qb, kb, gk, vb, s0, seg, *, chunk, tblock):
    nseq, t, _ = qb.shape
    assert t % tblock == 0 and tblock % chunk == 0
    ntb = t // tblock
    tok = lambda w: pl.BlockSpec((1, tblock, w), lambda s, tb: (s, tb, 0))
    st_spec = pl.BlockSpec((1, D_BK, DV_B), lambda s, tb: (s, 0, 0))
    return pl.pallas_call(
        functools.partial(_gla_kernel, chunk),
        grid=(nseq, ntb),
        in_specs=[tok(D_BK), tok(D_BK), tok(D_BK), tok(D_BV), st_spec,
                  pl.BlockSpec(seg.shape, lambda s, tb: (0, 0))],
        out_specs=[tok(D_BV), st_spec],
        out_shape=[jax.ShapeDtypeStruct((nseq, t, D_BV), jnp.float32),
                   jax.ShapeDtypeStruct((nseq, D_BK, DV_B), jnp.float32)],
        scratch_shapes=[pltpu.VMEM((D_BK, DV_B), jnp.float32)],
        compiler_params=pltpu.CompilerParams(dimension_semantics=("arbitrary", "arbitrary"),
                                             vmem_limit_bytes=VMEM_LIMIT),
        name="gla_c%d" % chunk,
    )(qb, kb, gk, vb, s0, seg)


def _post_kernel(x_ref, oa_ref, ob_ref, gb_ref, gn_ref, wo_ref, nm_ref, wu_ref, wd_ref, nf_ref,
                 y_ref, x1_ref, h2_ref, acc_ref):
    kf = pl.program_id(1)

    @pl.when(kf == 0)
    def _():
        ob = ob_ref[...]
        gb = gb_ref[...]
        gn = gn_ref[...]
        heads = []
        for h in range(N_HEADS_B):
            sl = slice(h * DV_B, (h + 1) * DV_B)
            gate = gb[:, sl]
            heads.append(_rms(ob[:, sl], gn[:, sl]) * (gate * (1.0 / (1.0 + jnp.exp(-gate)))))
        mix = _bf(jnp.concatenate([oa_ref[...]] + heads, axis=1))
        x1 = x_ref[...] + jnp.dot(mix, wo_ref[...], preferred_element_type=jnp.float32)
        x1_ref[...] = x1
        h2_ref[...] = _bf(_rms(x1, nm_ref[...]))
        acc_ref[...] = jnp.zeros_like(acc_ref)

    u = jnp.maximum(jnp.dot(h2_ref[...], wu_ref[...], preferred_element_type=jnp.float32), 0.0)
    acc_ref[...] += jnp.dot(_bf(u * u), wd_ref[...], preferred_element_type=jnp.float32)

    @pl.when(kf == pl.num_programs(1) - 1)
    def _():
        y_ref[...] = _rms(x1_ref[...] + acc_ref[...], nf_ref[...])


def _post(x2d, oa, ob, gb, gla_norm, w_o, norm_mlp, w_up, w_down, norm_final):
    n = x2d.shape[0]
    tm, tf = TOKEN_TILE, FF_TILE
    assert n % tm == 0 and D_FF % tf == 0
    row = lambda w: pl.BlockSpec((tm, w), lambda i, kf: (i, 0))
    full = lambda a: pl.BlockSpec(a.shape, lambda i, kf: (0,) * a.ndim)
    return pl.pallas_call(
        _post_kernel,
        grid=(n // tm, D_FF // tf),
        in_specs=[row(D_MODEL), row(D_A), row(D_BV), row(D_BV), full(gla_norm), full(w_o),
                  full(norm_mlp),
                  pl.BlockSpec((D_MODEL, tf), lambda i, kf: (0, kf)),
                  pl.BlockSpec((tf, D_MODEL), lambda i, kf: (kf, 0)),
                  full(norm_final)],
        out_specs=row(D_MODEL),
        out_shape=jax.ShapeDtypeStruct((n, D_MODEL), jnp.float32),
        scratch_shapes=[pltpu.VMEM((tm, D_MODEL), jnp.float32),
                        pltpu.VMEM((tm, D_MODEL), jnp.bfloat16),
                        pltpu.VMEM((tm, D_MODEL), jnp.float32)],
        compiler_params=pltpu.CompilerParams(dimension_semantics=("arbitrary", "arbitrary"),
                                             vmem_limit_bytes=VMEM_LIMIT),
        name="post",
    )(x2d, oa, ob, gb, gla_norm, w_o, norm_mlp, w_up, w_down, norm_final)


def _seg_matrix():
    d = np.arange(D_BK)[:, None] // DK_B
    e = np.arange(D_BV)[None, :] // DV_B
    return jnp.asarray(d == e, jnp.bfloat16)


def _layer(l, x_prompt, x_sample, cache_k, cache_v, state_gla, page_table, norm_mix, w_in, w_gk2,
           b_gk, gla_norm, w_o, norm_mlp, w_up, w_down):
    b, t_p, _ = x_prompt.shape
    db, t_s, _ = x_sample.shape
    page = cache_k.shape[3]
    n_pages = page_table.shape[1]
    past = n_pages * page
    ppb = MOBA_BLOCK // page
    assert past % MOBA_BLOCK == 0 and past // MOBA_BLOCK >= MOBA_TOPK
    assert t_s <= 8 and TOKEN_TILE % t_s == 0

    w_in_p = _bf(jnp.concatenate(
        [w_in[l], jnp.zeros((D_MODEL, D_IN_PAD - D_IN), w_in.dtype)], axis=1))
    w_gk2_p = _bf(jnp.concatenate(
        [w_gk2[l], jnp.zeros((LANES - GATE_RANK, D_BK), w_gk2.dtype)], axis=0))
    nmix = norm_mix[l].reshape(1, D_MODEL)
    bg = b_gk[l].reshape(1, D_BK)
    gn = gla_norm[l].reshape(1, D_BV)
    nmlp = norm_mlp[l].reshape(1, D_MODEL)
    wo, wu, wd = _bf(w_o[l]), _bf(w_up[l]), _bf(w_down[l])
    seg = _seg_matrix()

    xp2 = x_prompt.reshape(b * t_p, D_MODEL)
    tab_p = _rope_tables(jnp.arange(t_p, dtype=jnp.int32))
    (qs, qf, kbf, vt, k_p, v_p, means, qb, kb, vb, gb, gk) = _in_proj(
        xp2, nmix, w_in_p, w_gk2_p, bg, tab_p, sample=False, nseq=b, seq_len=t_p)
    nb = t_p // MOBA_BLOCK
    oa_p = _moba_prompt(qs.reshape(b, t_p, D_A), qf.reshape(b, t_p, D_A), kbf.reshape(b, t_p, D_A),
                        vt, means.reshape(b, nb, D_A))
    s0_p = jnp.zeros((b, D_BK, DV_B), state_gla.dtype)
    ob_p, s_p = _gla(qb.reshape(b, t_p, D_BK), kb.reshape(b, t_p, D_BK), gk.reshape(b, t_p, D_BK),
                     vb.reshape(b, t_p, D_BV), s0_p, seg, chunk=GLA_CHUNK_PROMPT,
                     tblock=min(GLA_TBLOCK, t_p))
    prompt = (xp2, oa_p.reshape(b * t_p, D_A), ob_p.reshape(b * t_p, D_BV), gb)

    xs2 = x_sample.reshape(db * t_s, D_MODEL)
    pos_s = past + jnp.tile(jnp.arange(t_s, dtype=jnp.int32), db)
    tab_s = _rope_tables(pos_s)
    (q_s, k_s, v_s, qb_s, kb_s, vb_s, gb_s, gk_s) = _in_proj(
        xs2, nmix, w_in_p, w_gk2_p, bg, tab_s, sample=True, nseq=db, seq_len=t_s)
    ids = _sample_select(l, page_table, q_s, cache_k)
    ids = ids[:, :, :MOBA_TOPK, :t_s]
    logical = ids.transpose(0, 1, 3, 2)[..., None] * ppb + jnp.arange(ppb, dtype=jnp.int32)
    phys = jnp.take_along_axis(page_table[:, None, :], logical.reshape(db, N_HEADS_A, -1), axis=2)
    npg = t_s * MOBA_TOPK * ppb
    oa_s = _sample_attend(l, phys.reshape(-1), q_s, k_s, v_s, cache_k, cache_v, npg)
    oa_s = oa_s.transpose(0, 2, 1, 3).reshape(db * t_s, D_A)
    ob_s, s_s = _gla(qb_s.reshape(db, t_s, D_BK), kb_s.reshape(db, t_s, D_BK),
                     gk_s.reshape(db, t_s, D_BK), vb_s.reshape(db, t_s, D_BV),
                     state_gla[l].reshape(db, D_BK, DV_B), seg, chunk=t_s, tblock=t_s)
    sample = (xs2, oa_s, ob_s.reshape(db * t_s, D_BV), gb_s)
    return prompt, sample, (gn, wo, nmlp, wu, wd), (k_p, v_p, s_p, k_s, v_s, s_s)


def kernel(x_prompt, x_sample, cache_k, cache_v, state_gla, page_table, norm_mix, w_in, w_gk2, b_gk,
           gla_norm, w_o, norm_mlp, w_up, w_down, norm_final):
    depth = norm_mix.shape[0]
    assert depth == 1, "the trunk is one layer deep"
    b, t_p, _ = x_prompt.shape
    db, t_s, _ = x_sample.shape
    prompt, sample, weights, new = _layer(0, x_prompt, x_sample, cache_k, cache_v, state_gla,
                                          page_table, norm_mix, w_in, w_gk2, b_gk, gla_norm, w_o,
                                          norm_mlp, w_up, w_down)
    nf = norm_final.reshape(1, D_MODEL)
    y_p = _post(*prompt, *weights, nf).reshape(b, t_p, D_MODEL)
    y_s = _post(*sample, *weights, nf).reshape(db, t_s, D_MODEL)
    k_p, v_p, s_p, k_s, v_s, s_s = new
    st = lambda s: s.reshape(1, s.shape[0], N_HEADS_B, DK_B, DV_B)
    return (y_p, y_s, k_p[None], v_p[None], st(s_p), k_s[None], v_s[None], st(s_s))
```

```python
import functools

import jax
import jax.numpy as jnp
import numpy as np
from jax import lax
from jax.experimental import pallas as pl
from jax.experimental.pallas import tpu as pltpu

D_MODEL = 1024
HEAD_DIM_A = 64
D_A = D_MODEL // 2
N_HEADS_A = D_A // HEAD_DIM_A
MOBA_BLOCK = 256
MOBA_TOPK = 3
ROPE_DIM = HEAD_DIM_A // 4
ROPE_THETA = 500000.0
D_BV = D_MODEL // 2
N_HEADS_B = 4
DV_B = D_BV // N_HEADS_B
DK_B = DV_B // 2
D_BK = N_HEADS_B * DK_B
GATE_RANK = 16
GATE_NORMALIZER = 16.0
D_IN = 3 * D_A + 2 * D_BK + 2 * D_BV + GATE_RANK
D_FF = 4 * D_MODEL
EPS = 1e-6
NEG = -1e30

LANES = 128
D_IN_PAD = D_IN - GATE_RANK + LANES
_OFF_QA, _OFF_KA, _OFF_VA = 0, D_A, 2 * D_A
_OFF_QB = 3 * D_A
_OFF_KB = _OFF_QB + D_BK
_OFF_VB = _OFF_KB + D_BK
_OFF_GB = _OFF_VB + D_BV
_OFF_LR = _OFF_GB + D_BV

TOKEN_TILE = 256
POST_TILE = 512
GLA_CHUNK_PROMPT = 16
GLA_TBLOCK = 256
GLA_SEQS_PER_STEP = 8
FF_TILE = 1024
SAMPLE_HEAD_GROUP = 2
VMEM_LIMIT = 48 * 1024 * 1024

_HI = lax.Precision.HIGHEST
_NT = (((1,), (1,)), ((), ()))
_TN = (((0,), (0,)), ((), ()))


def _bf(x):
    return x.astype(jnp.bfloat16)


def _rms(x, g):
    r = lax.rsqrt(jnp.mean(x * x, axis=-1, keepdims=True) + EPS)
    return (x * r) * g


def _rope(z, cos, sin_lo, sin_hi):
    n = z.shape[1]
    cos = jnp.concatenate([cos] * (n // LANES), axis=1)
    sin_lo = jnp.concatenate([sin_lo] * (n // LANES), axis=1)
    sin_hi = jnp.concatenate([sin_hi] * (n // LANES), axis=1)
    half = ROPE_DIM // 2
    up = pltpu.roll(z, n - half, 1)
    dn = pltpu.roll(z, half, 1)
    return z * cos + up * sin_lo + dn * sin_hi


def _in_proj_kernel(sample, x_ref, g_ref, w_ref, wg_ref, bg_ref, cos_ref, slo_ref, shi_ref, *outs):
    tm = x_ref.shape[0]
    hb = _bf(_rms(x_ref[...], g_ref[...]))

    def proj(off, width):
        return jnp.dot(hb, w_ref[:, off:off + width], preferred_element_type=jnp.float32)

    cos, slo, shi = cos_ref[...], slo_ref[...], shi_ref[...]
    q = _rope(proj(_OFF_QA, D_A), cos, slo, shi)
    k = _rope(proj(_OFF_KA, D_A), cos, slo, shi)
    v = proj(_OFF_VA, D_A)
    if sample:
        (q_hm, k_hm, v_hm, qb_o, kb_o, vb_o, gb_o, gk_o) = outs
        nseq = q_hm.shape[0]
        for h in range(N_HEADS_A):
            sl = slice(h * HEAD_DIM_A, (h + 1) * HEAD_DIM_A)
            q_hm[:, h] = q[:, sl].reshape(nseq, tm // nseq, HEAD_DIM_A)
            k_hm[:, h] = k[:, sl].reshape(nseq, tm // nseq, HEAD_DIM_A)
            v_hm[:, h] = v[:, sl].reshape(nseq, tm // nseq, HEAD_DIM_A)
    else:
        (qs_o, qf_o, kbf_o, vtb_o, kt_o, vt_o, mean_o, qb_o, kb_o, vb_o, gb_o, gk_o) = outs
        qs_o[...] = _bf(q * (HEAD_DIM_A ** -0.5))
        qf_o[...] = q
        kbf_o[...] = _bf(k)
        vt = v.T
        kt_o[0] = k.T
        vt_o[0] = vt
        vtb_o[0, 0] = _bf(vt)
        mean_o[0, 0] = jnp.sum(k, axis=0, keepdims=True) * (1.0 / tm)
    qb_o[...] = proj(_OFF_QB, D_BK) * (DK_B ** -0.5)
    kb_o[...] = proj(_OFF_KB, D_BK)
    vb_o[...] = proj(_OFF_VB, D_BV)
    gb_o[...] = proj(_OFF_GB, D_BV)
    lr = _bf(proj(_OFF_LR, LANES))
    zg = jnp.dot(lr, wg_ref[...], preferred_element_type=jnp.float32) + bg_ref[...]
    log_sig = jnp.minimum(zg, 0.0) - jnp.log1p(jnp.exp(-jnp.abs(zg)))
    gk_o[...] = log_sig * (1.0 / GATE_NORMALIZER)


def _in_proj(x2d, norm_mix, w_in_p, w_gk2_p, b_gk, tables, *, sample, nseq, seq_len):
    n = x2d.shape[0]
    tm = TOKEN_TILE
    assert n % tm == 0
    nt = n // tm
    cos, slo, shi = tables
    f32 = jnp.float32
    row = lambda w: pl.BlockSpec((tm, w), lambda i: (i, 0))
    full = lambda a: pl.BlockSpec(a.shape, lambda i: (0,) * a.ndim)
    gla_shapes = [jax.ShapeDtypeStruct((n, D_BK), f32), jax.ShapeDtypeStruct((n, D_BK), f32),
                  jax.ShapeDtypeStruct((n, D_BV), f32), jax.ShapeDtypeStruct((n, D_BV), f32),
                  jax.ShapeDtypeStruct((n, D_BK), f32)]
    gla_specs = [row(D_BK), row(D_BK), row(D_BV), row(D_BV), row(D_BK)]
    if sample:
        assert tm % seq_len == 0
        spt = tm // seq_len
        hm = jax.ShapeDtypeStruct((nseq, N_HEADS_A, seq_len, HEAD_DIM_A), f32)
        hm_spec = pl.BlockSpec((spt, N_HEADS_A, seq_len, HEAD_DIM_A), lambda i: (i, 0, 0, 0))
        out_shape = [hm, hm, hm] + gla_shapes
        out_specs = [hm_spec, hm_spec, hm_spec] + gla_specs
        tab_spec = row(LANES)
    else:
        assert seq_len % tm == 0 and tm == MOBA_BLOCK
        tps = seq_len // tm
        nb = tps
        chan_major = jax.ShapeDtypeStruct((nseq, D_A, seq_len), f32)
        chan_spec = pl.BlockSpec((1, D_A, tm), lambda i: (i // tps, 0, i % tps))
        out_shape = [jax.ShapeDtypeStruct((n, D_A), jnp.bfloat16),
                     jax.ShapeDtypeStruct((n, D_A), f32),
                     jax.ShapeDtypeStruct((n, D_A), jnp.bfloat16),
                     jax.ShapeDtypeStruct((nseq, nb, D_A, tm), jnp.bfloat16),
                     chan_major, chan_major,
                     jax.ShapeDtypeStruct((nseq, nb, 1, D_A), f32)] + gla_shapes
        out_specs = [row(D_A), row(D_A), row(D_A),
                     pl.BlockSpec((1, 1, D_A, tm), lambda i: (i // tps, i % tps, 0, 0)),
                     chan_spec, chan_spec,
                     pl.BlockSpec((1, 1, 1, D_A), lambda i: (i // tps, i % tps, 0, 0))] + gla_specs
        tab_spec = pl.BlockSpec((tm, LANES), lambda i: (i % tps, 0))
    return pl.pallas_call(
        functools.partial(_in_proj_kernel, sample),
        grid=(nt,),
        in_specs=[row(D_MODEL), full(norm_mix), full(w_in_p), full(w_gk2_p), full(b_gk),
                  tab_spec, tab_spec, tab_spec],
        out_specs=out_specs,
        out_shape=out_shape,
        compiler_params=pltpu.CompilerParams(dimension_semantics=("arbitrary",),
                                             vmem_limit_bytes=VMEM_LIMIT),
        name="in_proj_sample" if sample else "in_proj_prompt",
    )(x2d, norm_mix, w_in_p, w_gk2_p, b_gk, cos, slo, shi)


def _rope_tables(pos):
    half = ROPE_DIM // 2
    inv = ROPE_THETA ** (-jnp.arange(half, dtype=jnp.float32) / half)
    ang = pos.astype(jnp.float32)[:, None] * inv[None, :]
    c, s = jnp.cos(ang), jnp.sin(ang)
    n = pos.shape[0]
    ones = jnp.ones((n, HEAD_DIM_A - ROPE_DIM), jnp.float32)
    zeros = jnp.zeros((n, HEAD_DIM_A - ROPE_DIM), jnp.float32)
    zh = jnp.zeros((n, half), jnp.float32)
    cos = jnp.concatenate([c, c, ones], axis=1)
    sin_lo = jnp.concatenate([-s, zh, zeros], axis=1)
    sin_hi = jnp.concatenate([zh, s, zeros], axis=1)
    rep = LANES // HEAD_DIM_A
    return tuple(jnp.concatenate([t] * rep, axis=1) for t in (cos, sin_lo, sin_hi))


def _rank_rows(s, blk, nb):
    rank = jnp.zeros(s.shape, jnp.float32)
    for jp in range(nb):
        row = s[jp:jp + 1, :]
        before = (row > s) | ((row == s) & (blk > jp))
        rank = rank + jnp.where(before, 1.0, 0.0)
    return rank


def _moba_kernel(qs_ref, qf_ref, k_ref, vt_ref, mean_ref, o_ref, qm_ref, sel_ref, m_ref, l_ref,
                 acc_ref):
    i = pl.program_id(1)
    bq = qs_ref.shape[1]
    nb = mean_ref.shape[1]
    hd = HEAD_DIM_A
    nh = N_HEADS_A
    lane = lax.broadcasted_iota(jnp.int32, (bq, LANES), 1)
    halves = (lane < hd, lane >= hd)
    blk = lax.broadcasted_iota(jnp.int32, (nb, bq), 0)
    pair = lambda h: slice((h // 2) * LANES, (h // 2 + 1) * LANES)

    for h in range(nh):
        q2 = qs_ref[0, :, pair(h)]
        qm_ref[h] = jnp.where(halves[h % 2], q2, jnp.zeros_like(q2))
        qfh = jnp.where(halves[h % 2], qf_ref[0, :, pair(h)], 0.0)
        s = lax.dot_general(mean_ref[0, :, pair(h)], qfh, _NT, precision=_HI,
                            preferred_element_type=jnp.float32)
        s = jnp.where(blk < i, s, NEG)
        rank = _rank_rows(s, blk, nb)
        sel_ref[h] = jnp.where((rank < MOBA_TOPK) & (blk < i), 1.0, 0.0)

    kidx = lax.broadcasted_iota(jnp.int32, (bq, bq), 0)
    qidx = lax.broadcasted_iota(jnp.int32, (bq, bq), 1)
    row0 = pl.multiple_of(i * bq, bq)
    for h in range(nh):
        kd = k_ref[0, pl.ds(row0, bq), pair(h)]
        s = lax.dot_general(kd, qm_ref[h], _NT, preferred_element_type=jnp.float32)
        s = jnp.where(kidx <= qidx, s, NEG)
        m = jnp.max(s, axis=0, keepdims=True)
        p = jnp.exp(s - m)
        m_ref[h] = m
        l_ref[h] = jnp.sum(p, axis=0, keepdims=True)
        acc_ref[h] = jnp.dot(vt_ref[0, i, h * hd:(h + 1) * hd, :], _bf(p),
                             preferred_element_type=jnp.float32)

    def past(j, carry):
        rowj = pl.multiple_of(j * bq, bq)
        for h in range(nh):
            kj = k_ref[0, pl.ds(rowj, bq), pair(h)]
            s = lax.dot_general(kj, qm_ref[h], _NT, preferred_element_type=jnp.float32)
            s = jnp.where(sel_ref[h, pl.ds(j, 1), :] > 0.0, s, NEG)
            m_old = m_ref[h]
            m_new = jnp.maximum(m_old, jnp.max(s, axis=0, keepdims=True))
            alpha = jnp.exp(m_old - m_new)
            p = jnp.exp(s - m_new)
            m_ref[h] = m_new
            l_ref[h] = alpha * l_ref[h] + jnp.sum(p, axis=0, keepdims=True)
            acc_ref[h] = alpha * acc_ref[h] + jnp.dot(vt_ref[0, j, h * hd:(h + 1) * hd, :], _bf(p),
                                                      preferred_element_type=jnp.float32)
        return carry

    lax.fori_loop(0, i, past, 0)
    for p2 in range(nh // 2):
        ot = jnp.concatenate([acc_ref[2 * p2] / l_ref[2 * p2],
                              acc_ref[2 * p2 + 1] / l_ref[2 * p2 + 1]], axis=0)
        o_ref[0, :, p2 * LANES:(p2 + 1) * LANES] = ot.T


def _moba_prompt(qs, qf, kbf, vtb, means):
    b, t, _ = qs.shape
    nb = t // MOBA_BLOCK
    bq = MOBA_BLOCK
    return pl.pallas_call(
        _moba_kernel,
        grid=(b, nb),
        in_specs=[pl.BlockSpec((1, bq, D_A), lambda b_, i: (b_, i, 0)),
                  pl.BlockSpec((1, bq, D_A), lambda b_, i: (b_, i, 0)),
                  pl.BlockSpec((1, t, D_A), lambda b_, i: (b_, 0, 0)),
                  pl.BlockSpec((1, nb, D_A, bq), lambda b_, i: (b_, 0, 0, 0)),
                  pl.BlockSpec((1, nb, D_A), lambda b_, i: (b_, 0, 0))],
        out_specs=pl.BlockSpec((1, bq, D_A), lambda b_, i: (b_, i, 0)),
        out_shape=jax.ShapeDtypeStruct((b, t, D_A), jnp.float32),
        scratch_shapes=[pltpu.VMEM((N_HEADS_A, bq, LANES), jnp.bfloat16),
                        pltpu.VMEM((N_HEADS_A, nb, bq), jnp.float32),
                        pltpu.VMEM((N_HEADS_A, 1, bq), jnp.float32),
                        pltpu.VMEM((N_HEADS_A, 1, bq), jnp.float32),
                        pltpu.VMEM((N_HEADS_A, HEAD_DIM_A, bq), jnp.float32)],
        compiler_params=pltpu.CompilerParams(dimension_semantics=("arbitrary", "arbitrary"),
                                             vmem_limit_bytes=VMEM_LIMIT),
        name="moba_prompt",
    )(qs, qf, kbf, vtb, means)


def _pages_copy(cache_ref, layer, page, head0, buf, slot, pg, sem):
    hg = buf.shape[1]
    return pltpu.make_async_copy(cache_ref.at[layer, page, pl.ds(head0, hg)],
                                 buf.at[slot, :, :, pl.ds(pg * LANES, LANES)], sem)


def _moba_sample_kernel(layer, nseq, n_pages, pt_ref, q_ref, kn_ref, vn_ref, ck_ref, cv_ref, o_ref,
                        kbuf, vbuf, sem):
    b, g = pl.program_id(0), pl.program_id(1)
    ngrp = pl.num_programs(1)
    hg, hd = kbuf.shape[1], kbuf.shape[2]
    nkey = kbuf.shape[3]
    nblk = nkey // MOBA_BLOCK
    ppb = MOBA_BLOCK // LANES
    step = b * ngrp + g
    slot = step % 2

    def issue(bb, gg, sl):
        for pg in range(n_pages):
            page = pt_ref[bb * n_pages + pg]
            _pages_copy(ck_ref, layer, page, gg * hg, kbuf, sl, pg, sem.at[0, sl]).start()
            _pages_copy(cv_ref, layer, page, gg * hg, vbuf, sl, pg, sem.at[1, sl]).start()

    @pl.when(step == 0)
    def _():
        issue(b, g, slot)

    nxt = step + 1

    @pl.when(nxt < nseq * ngrp)
    def _():
        issue(nxt // ngrp, nxt % ngrp, 1 - slot)

    for pg in range(n_pages):
        _pages_copy(ck_ref, layer, 0, 0, kbuf, slot, pg, sem.at[0, slot]).wait()
        _pages_copy(cv_ref, layer, 0, 0, vbuf, slot, pg, sem.at[1, slot]).wait()

    ts = q_ref.shape[2]
    lane = lax.broadcasted_iota(jnp.int32, (ts, LANES), 1)
    trow = lax.broadcasted_iota(jnp.int32, (ts, LANES), 0)
    mlane = lax.broadcasted_iota(jnp.int32, (hd, LANES), 1)
    pad = jnp.zeros((LANES - ts, hd), jnp.float32)
    for h in range(hg):
        kt = kbuf[slot, h]
        mt = jnp.zeros((hd, LANES), jnp.float32)
        for j in range(nblk):
            x = kt[:, j * MOBA_BLOCK:j * MOBA_BLOCK + LANES]
            for pp in range(1, ppb):
                x = x + kt[:, j * MOBA_BLOCK + pp * LANES:j * MOBA_BLOCK + (pp + 1) * LANES]
            mt = jnp.where(mlane == j, jnp.sum(x, axis=1, keepdims=True), mt)
        mt = mt * (1.0 / MOBA_BLOCK)
        q = q_ref[0, h]
        sb = jnp.dot(q, mt, precision=_HI, preferred_element_type=jnp.float32)
        sb = jnp.where(lane < nblk, sb, NEG)
        rank = jnp.zeros(sb.shape, jnp.float32)
        for jp in range(nblk):
            col = sb[:, jp:jp + 1]
            before = (col > sb) | ((col == sb) & (lane > jp))
            rank = rank + jnp.where(before, 1.0, 0.0)
        sel = jnp.where((rank < MOBA_TOPK) & (lane < nblk), 1.0, 0.0)
        qb = _bf(q * (hd ** -0.5))
        s = jnp.dot(qb, _bf(kt), preferred_element_type=jnp.float32)
        s = jnp.concatenate(
            [jnp.where(sel[:, j:j + 1] > 0.0, s[:, j * MOBA_BLOCK:(j + 1) * MOBA_BLOCK], NEG)
             for j in range(nblk)], axis=1)
        kn = _bf(jnp.concatenate([kn_ref[0, h], pad], axis=0))
        vn = _bf(jnp.concatenate([vn_ref[0, h], pad], axis=0))
        so = lax.dot_general(qb, kn, _NT, preferred_element_type=jnp.float32)
        so = jnp.where(lane <= trow, so, NEG)
        m = jnp.maximum(jnp.max(s, axis=1, keepdims=True), jnp.max(so, axis=1, keepdims=True))
        p = jnp.exp(s - m)
        po = jnp.exp(so - m)
        l = jnp.sum(p, axis=1, keepdims=True) + jnp.sum(po, axis=1, keepdims=True)
        o = (lax.dot_general(_bf(p), _bf(vbuf[slot, h]), _NT, preferred_element_type=jnp.float32)
             + jnp.dot(_bf(po), vn, preferred_element_type=jnp.float32))
        o_ref[0, h] = o / l


def _moba_sample(layer, page_table, q_hm, k_hm, v_hm, cache_kt, cache_vt):
    db, nh, ts, hd = q_hm.shape
    n_pages = page_table.shape[1]
    page = cache_kt.shape[4]
    hg = SAMPLE_HEAD_GROUP
    assert page == LANES and MOBA_BLOCK % page == 0 and nh % hg == 0
    assert (n_pages * page) % MOBA_BLOCK == 0 and n_pages * page // MOBA_BLOCK <= LANES
    blk = pl.BlockSpec((1, hg, ts, hd), lambda b, g, pt: (b, g, 0, 0))
    grid_spec = pltpu.PrefetchScalarGridSpec(
        num_scalar_prefetch=1,
        grid=(db, nh // hg),
        in_specs=[blk, blk, blk, pl.BlockSpec(memory_space=pl.ANY), pl.BlockSpec(memory_space=pl.ANY)],
        out_specs=blk,
        scratch_shapes=[pltpu.VMEM((2, hg, hd, n_pages * page), jnp.float32),
                        pltpu.VMEM((2, hg, hd, n_pages * page), jnp.float32),
                        pltpu.SemaphoreType.DMA((2, 2))])
    return pl.pallas_call(
        functools.partial(_moba_sample_kernel, layer, db, n_pages),
        grid_spec=grid_spec,
        out_shape=jax.ShapeDtypeStruct((db, nh, ts, hd), jnp.float32),
        compiler_params=pltpu.CompilerParams(dimension_semantics=("arbitrary", "arbitrary"),
                                             vmem_limit_bytes=VMEM_LIMIT),
        name="moba_sample",
    )(page_table.reshape(-1), q_hm, k_hm, v_hm, cache_kt, cache_vt)


def _gla_chunk(c, q, k, g, v, st, seg, ltri, ones, trow):
    dk, dv = DK_B, DV_B
    cum = jnp.dot(ltri, g, precision=_HI, preferred_element_type=jnp.float32)
    tot = cum[c - 1:c, :]
    qe = _bf(q * jnp.exp(cum))
    kd = _bf(k * jnp.exp(tot - cum))
    parts = []
    for s in range(c):
        e = jnp.exp(jnp.where(trow >= s, cum - cum[s:s + 1, :], NEG))
        parts.append(q * k[s:s + 1, :] * e)
    pmat = _bf(jnp.concatenate(parts, axis=0))
    r = jnp.dot(pmat, seg, preferred_element_type=jnp.float32)
    o = r[0:c] * v[0:1, :]
    for s in range(1, c):
        o = o + r[s * c:(s + 1) * c] * v[s:s + 1, :]
    inter = [jnp.dot(qe[:, h * dk:(h + 1) * dk], _bf(st[h * dk:(h + 1) * dk, :]),
                     preferred_element_type=jnp.float32) for h in range(N_HEADS_B)]
    decay = jnp.exp(lax.dot_general(g, ones, _TN, precision=_HI,
                                    preferred_element_type=jnp.float32))
    vb = _bf(v)
    upd = [lax.dot_general(kd[:, h * dk:(h + 1) * dk], vb[:, h * dv:(h + 1) * dv], _TN,
                           preferred_element_type=jnp.float32) for h in range(N_HEADS_B)]
    return o + jnp.concatenate(inter, axis=1), st * decay + jnp.concatenate(upd, axis=0)


def _gla_kernel(chunk, q_ref, k_ref, g_ref, v_ref, s0_ref, seg_ref, o_ref, sout_ref, s_ref):
    tb = pl.program_id(1)
    c = chunk
    nsb = q_ref.shape[0]
    nchunk = q_ref.shape[1] // c

    @pl.when(tb == 0)
    def _():
        s_ref[...] = s0_ref[...]

    rows = lax.broadcasted_iota(jnp.int32, (c, c), 0)
    cols = lax.broadcasted_iota(jnp.int32, (c, c), 1)
    ltri = jnp.where(rows >= cols, 1.0, 0.0)
    ones = jnp.ones((c, DV_B), jnp.float32)
    trow = lax.broadcasted_iota(jnp.int32, (c, D_BK), 0)

    def body(ci, carry):
        t0 = pl.multiple_of(ci * c, c)
        for sq in range(nsb):
            tok = lambda ref: ref[sq, pl.ds(t0, c), :]
            o, st = _gla_chunk(c, tok(q_ref), tok(k_ref), tok(g_ref), tok(v_ref), s_ref[sq],
                               seg_ref[...], ltri, ones, trow)
            o_ref[sq, pl.ds(t0, c), :] = o
            s_ref[sq] = st
        return carry

    lax.fori_loop(0, nchunk, body, 0)

    @pl.when(tb == pl.num_programs(1) - 1)
    def _():
        sout_ref[...] = s_ref[...]


def _gla(qb, kb, gk, vb, s0, seg, *, chunk, tblock, nsb):
    nseq, t, _ = qb.shape
    assert t % tblock == 0 and tblock % chunk == 0 and nseq % nsb == 0
    ntb = t // tblock
    tok = lambda w: pl.BlockSpec((nsb, tblock, w), lambda s, tb: (s, tb, 0))
    st_spec = pl.BlockSpec((nsb, D_BK, DV_B), lambda s, tb: (s, 0, 0))
    return pl.pallas_call(
        functools.partial(_gla_kernel, chunk),
        grid=(nseq // nsb, ntb),
        in_specs=[tok(D_BK), tok(D_BK), tok(D_BK), tok(D_BV), st_spec,
                  pl.BlockSpec(seg.shape, lambda s, tb: (0, 0))],
        out_specs=[tok(D_BV), st_spec],
        out_shape=[jax.ShapeDtypeStruct((nseq, t, D_BV), jnp.float32),
                   jax.ShapeDtypeStruct((nseq, D_BK, DV_B), jnp.float32)],
        scratch_shapes=[pltpu.VMEM((nsb, D_BK, DV_B), jnp.float32)],
        compiler_params=pltpu.CompilerParams(dimension_semantics=("arbitrary", "arbitrary"),
                                             vmem_limit_bytes=VMEM_LIMIT),
        name="gla_c%d" % chunk,
    )(qb, kb, gk, vb, s0, seg)


def _post_kernel(x_ref, oa_ref, ob_ref, gb_ref, gn_ref, wo_ref, nm_ref, wu_ref, wd_ref, nf_ref,
                 y_ref, x1_ref, h2_ref, acc_ref):
    kf = pl.program_id(1)

    @pl.when(kf == 0)
    def _():
        ob = ob_ref[...]
        gb = gb_ref[...]
        gn = gn_ref[...]
        heads = []
        for h in range(N_HEADS_B):
            sl = slice(h * DV_B, (h + 1) * DV_B)
            gate = gb[:, sl]
            heads.append(_rms(ob[:, sl], gn[:, sl]) * (gate * (1.0 / (1.0 + jnp.exp(-gate)))))
        mix = _bf(jnp.concatenate([oa_ref[...]] + heads, axis=1))
        x1 = x_ref[...] + jnp.dot(mix, wo_ref[...], preferred_element_type=jnp.float32)
        x1_ref[...] = x1
        h2_ref[...] = _bf(_rms(x1, nm_ref[...]))
        acc_ref[...] = jnp.zeros_like(acc_ref)

    u = jnp.maximum(jnp.dot(h2_ref[...], wu_ref[...], preferred_element_type=jnp.float32), 0.0)
    acc_ref[...] += jnp.dot(_bf(u * u), wd_ref[...], preferred_element_type=jnp.float32)

    @pl.when(kf == pl.num_programs(1) - 1)
    def _():
        y_ref[...] = _rms(x1_ref[...] + acc_ref[...], nf_ref[...])


def _post(x2d, oa, ob, gb, gla_norm, w_o, norm_mlp, w_up, w_down, norm_final):
    n = x2d.shape[0]
    tm = POST_TILE if n % POST_TILE == 0 else TOKEN_TILE
    tf = FF_TILE
    assert n % tm == 0 and D_FF % tf == 0
    row = lambda w: pl.BlockSpec((tm, w), lambda i, kf: (i, 0))
    full = lambda a: pl.BlockSpec(a.shape, lambda i, kf: (0,) * a.ndim)
    return pl.pallas_call(
        _post_kernel,
        grid=(n // tm, D_FF // tf),
        in_specs=[row(D_MODEL), row(D_A), row(D_BV), row(D_BV), full(gla_norm), full(w_o),
                  full(norm_mlp),
                  pl.BlockSpec((D_MODEL, tf), lambda i, kf: (0, kf)),
                  pl.BlockSpec((tf, D_MODEL), lambda i, kf: (kf, 0)),
                  full(norm_final)],
        out_specs=row(D_MODEL),
        out_shape=jax.ShapeDtypeStruct((n, D_MODEL), jnp.float32),
        scratch_shapes=[pltpu.VMEM((tm, D_MODEL), jnp.float32),
                        pltpu.VMEM((tm, D_MODEL), jnp.bfloat16),
                        pltpu.VMEM((tm, D_MODEL), jnp.float32)],
        compiler_params=pltpu.CompilerParams(dimension_semantics=("arbitrary", "arbitrary"),
                                             vmem_limit_bytes=VMEM_LIMIT),
        name="post",
    )(x2d, oa, ob, gb, gla_norm, w_o, norm_mlp, w_up, w_down, norm_final)


def _seg_matrix():
    d = np.arange(D_BK)[:, None] // DK_B
    e = np.arange(D_BV)[None, :] // DV_B
    return jnp.asarray(d == e, jnp.bfloat16)


def _layer(l, x_prompt, x_sample, cache_k, cache_v, state_gla, page_table, norm_mix, w_in, w_gk2,
           b_gk, gla_norm, w_o, norm_mlp, w_up, w_down):
    b, t_p, _ = x_prompt.shape
    db, t_s, _ = x_sample.shape
    page = cache_k.shape[3]
    n_pages = page_table.shape[1]
    past = n_pages * page
    assert past % MOBA_BLOCK == 0 and past // MOBA_BLOCK >= MOBA_TOPK
    assert t_s <= 8 and TOKEN_TILE % t_s == 0

    w_in_p = _bf(jnp.concatenate(
        [w_in[l], jnp.zeros((D_MODEL, D_IN_PAD - D_IN), w_in.dtype)], axis=1))
    w_gk2_p = _bf(jnp.concatenate(
        [w_gk2[l], jnp.zeros((LANES - GATE_RANK, D_BK), w_gk2.dtype)], axis=0))
    nmix = norm_mix[l].reshape(1, D_MODEL)
    bg = b_gk[l].reshape(1, D_BK)
    gn = gla_norm[l].reshape(1, D_BV)
    nmlp = norm_mlp[l].reshape(1, D_MODEL)
    wo, wu, wd = _bf(w_o[l]), _bf(w_up[l]), _bf(w_down[l])
    seg = _seg_matrix()

    xp2 = x_prompt.reshape(b * t_p, D_MODEL)
    tab_p = _rope_tables(jnp.arange(t_p, dtype=jnp.int32))
    (qs, qf, kbf, vtb, kt_p, vt_p, means, qb, kb, vb, gb, gk) = _in_proj(
        xp2, nmix, w_in_p, w_gk2_p, bg, tab_p, sample=False, nseq=b, seq_len=t_p)
    nb = t_p // MOBA_BLOCK
    oa_p = _moba_prompt(qs.reshape(b, t_p, D_A), qf.reshape(b, t_p, D_A), kbf.reshape(b, t_p, D_A),
                        vtb, means.reshape(b, nb, D_A))
    s0_p = jnp.zeros((b, D_BK, DV_B), state_gla.dtype)
    ob_p, s_p = _gla(qb.reshape(b, t_p, D_BK), kb.reshape(b, t_p, D_BK), gk.reshape(b, t_p, D_BK),
                     vb.reshape(b, t_p, D_BV), s0_p, seg, chunk=GLA_CHUNK_PROMPT,
                     tblock=min(GLA_TBLOCK, t_p), nsb=b)
    prompt = (xp2, oa_p.reshape(b * t_p, D_A), ob_p.reshape(b * t_p, D_BV), gb)
    to_rows = lambda a: jnp.swapaxes(a.reshape(b, N_HEADS_A, HEAD_DIM_A, t_p), 2, 3)
    k_p, v_p = to_rows(kt_p), to_rows(vt_p)

    xs2 = x_sample.reshape(db * t_s, D_MODEL)
    pos_s = past + jnp.tile(jnp.arange(t_s, dtype=jnp.int32), db)
    tab_s = _rope_tables(pos_s)
    (q_s, k_s, v_s, qb_s, kb_s, vb_s, gb_s, gk_s) = _in_proj(
        xs2, nmix, w_in_p, w_gk2_p, bg, tab_s, sample=True, nseq=db, seq_len=t_s)
    oa_s = _moba_sample(l, page_table, q_s, k_s, v_s,
                        jnp.swapaxes(cache_k, 3, 4), jnp.swapaxes(cache_v, 3, 4))
    oa_s = oa_s.transpose(0, 2, 1, 3).reshape(db * t_s, D_A)
    nsb = GLA_SEQS_PER_STEP if db % GLA_SEQS_PER_STEP == 0 else 1
    ob_s, s_s = _gla(qb_s.reshape(db, t_s, D_BK), kb_s.reshape(db, t_s, D_BK),
                     gk_s.reshape(db, t_s, D_BK), vb_s.reshape(db, t_s, D_BV),
                     state_gla[l].reshape(db, D_BK, DV_B), seg, chunk=t_s, tblock=t_s, nsb=nsb)
    sample = (xs2, oa_s, ob_s.reshape(db * t_s, D_BV), gb_s)
    return prompt, sample, (gn, wo, nmlp, wu, wd), (k_p, v_p, s_p, k_s, v_s, s_s)


def kernel(x_prompt, x_sample, cache_k, cache_v, state_gla, page_table, norm_mix, w_in, w_gk2, b_gk,
           gla_norm, w_o, norm_mlp, w_up, w_down, norm_final):
    depth = norm_mix.shape[0]
    assert depth == 1, "the trunk is one layer deep"
    b, t_p, _ = x_prompt.shape
    db, t_s, _ = x_sample.shape
    prompt, sample, weights, new = _layer(0, x_prompt, x_sample, cache_k, cache_v, state_gla,
                                          page_table, norm_mix, w_in, w_gk2, b_gk, gla_norm, w_o,
                                          norm_mlp, w_up, w_down)
    nf = norm_final.reshape(1, D_MODEL)
    y_p = _post(*prompt, *weights, nf).reshape(b, t_p, D_MODEL)
    y_s = _post(*sample, *weights, nf).reshape(db, t_s, D_MODEL)
    k_p, v_p, s_p, k_s, v_s, s_s = new
    st = lambda s: s.reshape(1, s.shape[0], N_HEADS_B, DK_B, DV_B)
    return (y_p, y_s, k_p[None], v_p[None], st(s_p), k_s[None], v_s[None], st(s_s))
```

```python
import functools

import jax
import jax.numpy as jnp
import numpy as np
from jax import lax
from jax.experimental import pallas as pl
from jax.experimental.pallas import tpu as pltpu

D_MODEL = 1024
HEAD_DIM_A = 64
D_A = D_MODEL // 2
N_HEADS_A = D_A // HEAD_DIM_A
MOBA_BLOCK = 256
MOBA_TOPK = 3
ROPE_DIM = HEAD_DIM_A // 4
ROPE_THETA = 500000.0
D_BV = D_MODEL // 2
N_HEADS_B = 4
DV_B = D_BV // N_HEADS_B
DK_B = DV_B // 2
D_BK = N_HEADS_B * DK_B
GATE_RANK = 16
GATE_NORMALIZER = 16.0
D_IN = 3 * D_A + 2 * D_BK + 2 * D_BV + GATE_RANK
D_FF = 4 * D_MODEL
EPS = 1e-6
NEG = -1e30
LOG2_E = 1.4426950408889634

LANES = 128
D_IN_PAD = D_IN - GATE_RANK + LANES
_OFF_QA, _OFF_KA, _OFF_VA = 0, D_A, 2 * D_A
_OFF_QB = 3 * D_A
_OFF_KB = _OFF_QB + D_BK
_OFF_VB = _OFF_KB + D_BK
_OFF_GB = _OFF_VB + D_BV
_OFF_LR = _OFF_GB + D_BV

TOKEN_TILE = 256
POST_TILE = 512
GLA_CHUNK_PROMPT = 16
GLA_TBLOCK = 256
GLA_SEQS_PER_STEP = 8
FF_TILE = 1024
SAMPLE_HEAD_GROUP = 2
VMEM_LIMIT = 48 * 1024 * 1024

_HI = lax.Precision.HIGHEST
_NT = (((1,), (1,)), ((), ()))
_TN = (((0,), (0,)), ((), ()))


def _bf(x):
    return x.astype(jnp.bfloat16)


def _rms(x, g):
    r = lax.rsqrt(jnp.mean(x * x, axis=-1, keepdims=True) + EPS)
    return (x * r) * g


def _rope(z, cos, sin_lo, sin_hi):
    n = z.shape[1]
    cos = jnp.concatenate([cos] * (n // LANES), axis=1)
    sin_lo = jnp.concatenate([sin_lo] * (n // LANES), axis=1)
    sin_hi = jnp.concatenate([sin_hi] * (n // LANES), axis=1)
    half = ROPE_DIM // 2
    up = pltpu.roll(z, n - half, 1)
    dn = pltpu.roll(z, half, 1)
    return z * cos + up * sin_lo + dn * sin_hi


def _in_proj_kernel(sample, x_ref, g_ref, w_ref, wg_ref, bg_ref, cos_ref, slo_ref, shi_ref, *outs):
    tm = x_ref.shape[0]
    hb = _bf(_rms(x_ref[...], g_ref[...]))

    def proj(off, width):
        return jnp.dot(hb, w_ref[:, off:off + width], preferred_element_type=jnp.float32)

    cos, slo, shi = cos_ref[...], slo_ref[...], shi_ref[...]
    q = _rope(proj(_OFF_QA, D_A), cos, slo, shi)
    k = _rope(proj(_OFF_KA, D_A), cos, slo, shi)
    v = proj(_OFF_VA, D_A)
    if sample:
        (q_hm, k_hm, v_hm, qb_o, kb_o, vb_o, gb_o, gk_o) = outs
        nseq = q_hm.shape[0]
        for h in range(N_HEADS_A):
            sl = slice(h * HEAD_DIM_A, (h + 1) * HEAD_DIM_A)
            q_hm[:, h] = q[:, sl].reshape(nseq, tm // nseq, HEAD_DIM_A)
            k_hm[:, h] = k[:, sl].reshape(nseq, tm // nseq, HEAD_DIM_A)
            v_hm[:, h] = v[:, sl].reshape(nseq, tm // nseq, HEAD_DIM_A)
    else:
        (qs_o, qf_o, kbf_o, vtb_o, kt_o, vt_o, mean_o, qb_o, kb_o, vb_o, gb_o, gk_o) = outs
        qs_o[...] = _bf(q * (HEAD_DIM_A ** -0.5 * LOG2_E))
        qf_o[...] = q
        kbf_o[...] = _bf(k)
        vt = v.T
        kt_o[0] = k.T
        vt_o[0] = vt
        vtb_o[0, 0] = _bf(vt)
        mean_o[0, 0] = jnp.sum(k, axis=0, keepdims=True) * (1.0 / tm)
    qb_o[...] = proj(_OFF_QB, D_BK) * (DK_B ** -0.5)
    kb_o[...] = proj(_OFF_KB, D_BK)
    vb_o[...] = proj(_OFF_VB, D_BV)
    gb_o[...] = proj(_OFF_GB, D_BV)
    lr = _bf(proj(_OFF_LR, LANES))
    zg = jnp.dot(lr, wg_ref[...], preferred_element_type=jnp.float32) + bg_ref[...]
    log_sig = jnp.minimum(zg, 0.0) - jnp.log1p(jnp.exp(-jnp.abs(zg)))
    gk_o[...] = log_sig * (1.0 / GATE_NORMALIZER)


def _in_proj(x2d, norm_mix, w_in_p, w_gk2_p, b_gk, tables, *, sample, nseq, seq_len):
    n = x2d.shape[0]
    tm = TOKEN_TILE
    assert n % tm == 0
    nt = n // tm
    cos, slo, shi = tables
    f32 = jnp.float32
    row = lambda w: pl.BlockSpec((tm, w), lambda i: (i, 0))
    full = lambda a: pl.BlockSpec(a.shape, lambda i: (0,) * a.ndim)
    gla_shapes = [jax.ShapeDtypeStruct((n, D_BK), f32), jax.ShapeDtypeStruct((n, D_BK), f32),
                  jax.ShapeDtypeStruct((n, D_BV), f32), jax.ShapeDtypeStruct((n, D_BV), f32),
                  jax.ShapeDtypeStruct((n, D_BK), f32)]
    gla_specs = [row(D_BK), row(D_BK), row(D_BV), row(D_BV), row(D_BK)]
    if sample:
        assert tm % seq_len == 0
        spt = tm // seq_len
        hm = jax.ShapeDtypeStruct((nseq, N_HEADS_A, seq_len, HEAD_DIM_A), f32)
        hm_spec = pl.BlockSpec((spt, N_HEADS_A, seq_len, HEAD_DIM_A), lambda i: (i, 0, 0, 0))
        out_shape = [hm, hm, hm] + gla_shapes
        out_specs = [hm_spec, hm_spec, hm_spec] + gla_specs
        tab_spec = row(LANES)
    else:
        assert seq_len % tm == 0 and tm == MOBA_BLOCK
        tps = seq_len // tm
        nb = tps
        chan_major = jax.ShapeDtypeStruct((nseq, D_A, seq_len), f32)
        chan_spec = pl.BlockSpec((1, D_A, tm), lambda i: (i // tps, 0, i % tps))
        out_shape = [jax.ShapeDtypeStruct((n, D_A), jnp.bfloat16),
                     jax.ShapeDtypeStruct((n, D_A), f32),
                     jax.ShapeDtypeStruct((n, D_A), jnp.bfloat16),
                     jax.ShapeDtypeStruct((nseq, nb, D_A, tm), jnp.bfloat16),
                     chan_major, chan_major,
                     jax.ShapeDtypeStruct((nseq, nb, 1, D_A), f32)] + gla_shapes
        out_specs = [row(D_A), row(D_A), row(D_A),
                     pl.BlockSpec((1, 1, D_A, tm), lambda i: (i // tps, i % tps, 0, 0)),
                     chan_spec, chan_spec,
                     pl.BlockSpec((1, 1, 1, D_A), lambda i: (i // tps, i % tps, 0, 0))] + gla_specs
        tab_spec = pl.BlockSpec((tm, LANES), lambda i: (i % tps, 0))
    return pl.pallas_call(
        functools.partial(_in_proj_kernel, sample),
        grid=(nt,),
        in_specs=[row(D_MODEL), full(norm_mix), full(w_in_p), full(w_gk2_p), full(b_gk),
                  tab_spec, tab_spec, tab_spec],
        out_specs=out_specs,
        out_shape=out_shape,
        compiler_params=pltpu.CompilerParams(dimension_semantics=("arbitrary",),
                                             vmem_limit_bytes=VMEM_LIMIT),
        name="in_proj_sample" if sample else "in_proj_prompt",
    )(x2d, norm_mix, w_in_p, w_gk2_p, b_gk, cos, slo, shi)


def _rope_tables(pos):
    half = ROPE_DIM // 2
    inv = ROPE_THETA ** (-jnp.arange(half, dtype=jnp.float32) / half)
    ang = pos.astype(jnp.float32)[:, None] * inv[None, :]
    c, s = jnp.cos(ang), jnp.sin(ang)
    n = pos.shape[0]
    ones = jnp.ones((n, HEAD_DIM_A - ROPE_DIM), jnp.float32)
    zeros = jnp.zeros((n, HEAD_DIM_A - ROPE_DIM), jnp.float32)
    zh = jnp.zeros((n, half), jnp.float32)
    cos = jnp.concatenate([c, c, ones], axis=1)
    sin_lo = jnp.concatenate([-s, zh, zeros], axis=1)
    sin_hi = jnp.concatenate([zh, s, zeros], axis=1)
    rep = LANES // HEAD_DIM_A
    return tuple(jnp.concatenate([t] * rep, axis=1) for t in (cos, sin_lo, sin_hi))


def _rank_rows(s, blk, nb):
    rank = jnp.zeros(s.shape, jnp.float32)
    for jp in range(nb):
        row = s[jp:jp + 1, :]
        before = (row > s) | ((row == s) & (blk > jp))
        rank = rank + jnp.where(before, 1.0, 0.0)
    return rank


def _moba_kernel(qs_ref, qf_ref, k_ref, vt_ref, mean_ref, oh_ref, o_ref, qm_ref, sel_ref, m_ref,
                 l_ref, acc_ref, s_ref, p_ref):
    i = pl.program_id(1)
    bq = qs_ref.shape[1]
    nb = mean_ref.shape[1]
    hd = HEAD_DIM_A
    nh = N_HEADS_A
    lane = lax.broadcasted_iota(jnp.int32, (bq, LANES), 1)
    halves = (lane < hd, lane >= hd)
    blk = lax.broadcasted_iota(jnp.int32, (nb, bq), 0)
    pair = lambda h: slice((h // 2) * LANES, (h // 2 + 1) * LANES)

    row0 = pl.multiple_of(i * bq, bq)
    for h in range(nh):
        q2 = qs_ref[0, :, pair(h)]
        qm_ref[h, :, 0:LANES] = jnp.where(halves[h % 2], q2, jnp.zeros_like(q2))
    for h in range(nh):
        kd = k_ref[0, pl.ds(row0, bq), pair(h)]
        s_ref[h] = lax.dot_general(kd, qm_ref[h, :, 0:LANES], _NT,
                                   preferred_element_type=jnp.float32)
    for h in range(nh):
        qfh = jnp.where(halves[h % 2], qf_ref[0, :, pair(h)], 0.0)
        sel_ref[h] = lax.dot_general(mean_ref[0, :, pair(h)], qfh, _NT, precision=_HI,
                                     preferred_element_type=jnp.float32)
    for h in range(nh):
        s = jnp.where(blk < i, sel_ref[h], NEG)
        rank = _rank_rows(s, blk, nb)
        bias = jnp.where((rank < MOBA_TOPK) & (blk < i), 0.0, NEG)
        bias = jnp.concatenate([bias, jnp.zeros((LANES - nb, bq), jnp.float32)], axis=0)
        qm_ref[h, :, LANES:2 * LANES] = _bf(bias.T)

    kidx = lax.broadcasted_iota(jnp.int32, (bq, bq), 0)
    qidx = lax.broadcasted_iota(jnp.int32, (bq, bq), 1)
    for h in range(nh):
        s = jnp.where(kidx <= qidx, s_ref[h], NEG)
        m = jnp.max(s, axis=0, keepdims=True)
        p = jnp.exp2(s - m)
        m_ref[h] = m
        l_ref[h] = jnp.sum(p, axis=0, keepdims=True)
        p_ref[h] = _bf(p)
    for h in range(nh):
        acc_ref[h] = jnp.dot(vt_ref[0, i, h * hd:(h + 1) * hd, :], p_ref[h],
                             preferred_element_type=jnp.float32)

    def past(j, carry):
        rowj = pl.multiple_of(j * bq, bq)
        onehot = oh_ref[j]
        for h in range(nh):
            kj = jnp.concatenate([k_ref[0, pl.ds(rowj, bq), pair(h)], onehot], axis=1)
            s_ref[h] = lax.dot_general(kj, qm_ref[h], _NT, preferred_element_type=jnp.float32)
        alphas = []
        for h in range(nh):
            m_old = m_ref[h]
            m_new = jnp.maximum(m_old, jnp.max(s_ref[h], axis=0, keepdims=True))
            alpha = jnp.exp2(m_old - m_new)
            p = jnp.exp2(s_ref[h] - m_new)
            m_ref[h] = m_new
            l_ref[h] = alpha * l_ref[h] + jnp.sum(p, axis=0, keepdims=True)
            p_ref[h] = _bf(p)
            alphas.append(alpha)
        for h in range(nh):
            acc_ref[h] = alphas[h] * acc_ref[h] + jnp.dot(
                vt_ref[0, j, h * hd:(h + 1) * hd, :], p_ref[h], preferred_element_type=jnp.float32)
        return carry

    lax.fori_loop(0, i, past, 0)
    for p2 in range(nh // 2):
        ot = jnp.concatenate([acc_ref[2 * p2] / l_ref[2 * p2],
                              acc_ref[2 * p2 + 1] / l_ref[2 * p2 + 1]], axis=0)
        o_ref[0, :, p2 * LANES:(p2 + 1) * LANES] = ot.T


def _moba_prompt(qs, qf, kbf, vtb, means):
    b, t, _ = qs.shape
    nb = t // MOBA_BLOCK
    bq = MOBA_BLOCK
    assert nb <= LANES
    onehot = jnp.asarray(np.arange(LANES)[None, None, :] == np.arange(nb)[:, None, None],
                         jnp.bfloat16) * jnp.ones((nb, bq, LANES), jnp.bfloat16)
    return pl.pallas_call(
        _moba_kernel,
        grid=(b, nb),
        in_specs=[pl.BlockSpec((1, bq, D_A), lambda b_, i: (b_, i, 0)),
                  pl.BlockSpec((1, bq, D_A), lambda b_, i: (b_, i, 0)),
                  pl.BlockSpec((1, t, D_A), lambda b_, i: (b_, 0, 0)),
                  pl.BlockSpec((1, nb, D_A, bq), lambda b_, i: (b_, 0, 0, 0)),
                  pl.BlockSpec((1, nb, D_A), lambda b_, i: (b_, 0, 0)),
                  pl.BlockSpec((nb, bq, LANES), lambda b_, i: (0, 0, 0))],
        out_specs=pl.BlockSpec((1, bq, D_A), lambda b_, i: (b_, i, 0)),
        out_shape=jax.ShapeDtypeStruct((b, t, D_A), jnp.float32),
        scratch_shapes=[pltpu.VMEM((N_HEADS_A, bq, 2 * LANES), jnp.bfloat16),
                        pltpu.VMEM((N_HEADS_A, nb, bq), jnp.float32),
                        pltpu.VMEM((N_HEADS_A, 1, bq), jnp.float32),
                        pltpu.VMEM((N_HEADS_A, 1, bq), jnp.float32),
                        pltpu.VMEM((N_HEADS_A, HEAD_DIM_A, bq), jnp.float32),
                        pltpu.VMEM((N_HEADS_A, bq, bq), jnp.float32),
                        pltpu.VMEM((N_HEADS_A, bq, bq), jnp.bfloat16)],
        compiler_params=pltpu.CompilerParams(dimension_semantics=("arbitrary", "arbitrary"),
                                             vmem_limit_bytes=VMEM_LIMIT),
        name="moba_prompt",
    )(qs, qf, kbf, vtb, means, onehot)


def _pages_copy(cache_ref, layer, page, head0, buf, slot, pg, sem):
    hg = buf.shape[1]
    return pltpu.make_async_copy(cache_ref.at[layer, page, pl.ds(head0, hg)],
                                 buf.at[slot, :, :, pl.ds(pg * LANES, LANES)], sem)


def _moba_sample_kernel(layer, nseq, n_pages, pt_ref, q_ref, kn_ref, vn_ref, ck_ref, cv_ref, o_ref,
                        kbuf, vbuf, sem, s_ref, p_ref):
    b, g = pl.program_id(0), pl.program_id(1)
    ngrp = pl.num_programs(1)
    hg, hd = kbuf.shape[1], kbuf.shape[2]
    nkey = kbuf.shape[3]
    nblk = nkey // MOBA_BLOCK
    ppb = MOBA_BLOCK // LANES
    step = b * ngrp + g
    last = nseq * ngrp - 1
    slot = step % 2

    def issue(bb, gg, sl, pg):
        page = pt_ref[bb * n_pages + pg]
        _pages_copy(ck_ref, layer, page, gg * hg, kbuf, sl, pg, sem.at[0, sl]).start()
        _pages_copy(cv_ref, layer, page, gg * hg, vbuf, sl, pg, sem.at[1, sl]).start()

    def wait_all(sl):
        for pg in range(n_pages):
            _pages_copy(ck_ref, layer, 0, 0, kbuf, sl, pg, sem.at[0, sl]).wait()
        for pg in range(n_pages):
            _pages_copy(cv_ref, layer, 0, 0, vbuf, sl, pg, sem.at[1, sl]).wait()

    @pl.when(step == 0)
    def _():
        for pg in range(n_pages):
            issue(b, g, slot, pg)

    wait_all(slot)
    nxt = jnp.minimum(step + 1, last)
    nxt_b, nxt_g = nxt // ngrp, nxt % ngrp

    ts = q_ref.shape[2]
    lane = lax.broadcasted_iota(jnp.int32, (ts, LANES), 1)
    trow = lax.broadcasted_iota(jnp.int32, (ts, LANES), 0)
    mlane = lax.broadcasted_iota(jnp.int32, (hd, LANES), 1)
    pad = jnp.zeros((LANES - ts, hd), jnp.float32)
    heads = range(hg)
    mts = []
    for h in heads:
        mt = jnp.zeros((hd, LANES), jnp.float32)
        for j in range(nblk):
            x = kbuf[slot, h, :, j * MOBA_BLOCK:j * MOBA_BLOCK + LANES]
            for pp in range(1, ppb):
                x = x + kbuf[slot, h, :, j * MOBA_BLOCK + pp * LANES:j * MOBA_BLOCK + (pp + 1) * LANES]
            mt = jnp.where(mlane == j, jnp.sum(x, axis=1, keepdims=True), mt)
            if h == 0:
                for pp in range(ppb):
                    issue(nxt_b, nxt_g, 1 - slot, j * ppb + pp)
        mts.append(mt * (1.0 / MOBA_BLOCK))
    qs = [q_ref[0, h] for h in heads]
    sbs = [jnp.where(lane < nblk,
                     jnp.dot(qs[h], mts[h], precision=_HI, preferred_element_type=jnp.float32), NEG)
           for h in heads]
    qbs = [_bf(q * (hd ** -0.5)) for q in qs]
    for h in heads:
        s_ref[h] = jnp.dot(qbs[h], _bf(kbuf[slot, h]), preferred_element_type=jnp.float32)
    sels = []
    for h in heads:
        rank = jnp.zeros((ts, LANES), jnp.float32)
        for jp in range(nblk):
            col = sbs[h][:, jp:jp + 1]
            before = (col > sbs[h]) | ((col == sbs[h]) & (lane > jp))
            rank = rank + jnp.where(before, 1.0, 0.0)
        sels.append(jnp.where((rank < MOBA_TOPK) & (lane < nblk), 1.0, 0.0))
    vns, pos, ls = [], [], []
    for h in heads:
        s = jnp.concatenate(
            [jnp.where(sels[h][:, j:j + 1] > 0.0, s_ref[h, :, j * MOBA_BLOCK:(j + 1) * MOBA_BLOCK],
                       NEG) for j in range(nblk)], axis=1)
        kn = _bf(jnp.concatenate([kn_ref[0, h], pad], axis=0))
        vns.append(_bf(jnp.concatenate([vn_ref[0, h], pad], axis=0)))
        so = lax.dot_general(qbs[h], kn, _NT, preferred_element_type=jnp.float32)
        so = jnp.where(lane <= trow, so, NEG)
        m = jnp.maximum(jnp.max(s, axis=1, keepdims=True), jnp.max(so, axis=1, keepdims=True))
        p = jnp.exp(s - m)
        po = jnp.exp(so - m)
        ls.append(jnp.sum(p, axis=1, keepdims=True) + jnp.sum(po, axis=1, keepdims=True))
        p_ref[h] = _bf(p)
        pos.append(_bf(po))
    for h in heads:
        o = (lax.dot_general(p_ref[h], _bf(vbuf[slot, h]), _NT, preferred_element_type=jnp.float32)
             + jnp.dot(pos[h], vns[h], preferred_element_type=jnp.float32))
        o_ref[0, h] = o / ls[h]

    @pl.when(step == last)
    def _():
        wait_all(1 - slot)


def _moba_sample(layer, page_table, q_hm, k_hm, v_hm, cache_kt, cache_vt):
    db, nh, ts, hd = q_hm.shape
    n_pages = page_table.shape[1]
    page = cache_kt.shape[4]
    hg = SAMPLE_HEAD_GROUP
    assert page == LANES and MOBA_BLOCK % page == 0 and nh % hg == 0
    assert (n_pages * page) % MOBA_BLOCK == 0 and n_pages * page // MOBA_BLOCK <= LANES
    blk = pl.BlockSpec((1, hg, ts, hd), lambda b, g, pt: (b, g, 0, 0))
    grid_spec = pltpu.PrefetchScalarGridSpec(
        num_scalar_prefetch=1,
        grid=(db, nh // hg),
        in_specs=[blk, blk, blk, pl.BlockSpec(memory_space=pl.ANY), pl.BlockSpec(memory_space=pl.ANY)],
        out_specs=blk,
        scratch_shapes=[pltpu.VMEM((2, hg, hd, n_pages * page), jnp.float32),
                        pltpu.VMEM((2, hg, hd, n_pages * page), jnp.float32),
                        pltpu.SemaphoreType.DMA((2, 2)),
                        pltpu.VMEM((hg, ts, n_pages * page), jnp.float32),
                        pltpu.VMEM((hg, ts, n_pages * page), jnp.bfloat16)])
    return pl.pallas_call(
        functools.partial(_moba_sample_kernel, layer, db, n_pages),
        grid_spec=grid_spec,
        out_shape=jax.ShapeDtypeStruct((db, nh, ts, hd), jnp.float32),
        compiler_params=pltpu.CompilerParams(dimension_semantics=("arbitrary", "arbitrary"),
                                             vmem_limit_bytes=VMEM_LIMIT),
        name="moba_sample",
    )(page_table.reshape(-1), q_hm, k_hm, v_hm, cache_kt, cache_vt)


def _gla_chunks(c, qs, ks, gs, vs, sts, seg, ltri, hmask, trow):
    dk, dv = DK_B, DV_B
    n = len(qs)
    heads = range(N_HEADS_B)
    f32 = jnp.float32
    zero = jnp.zeros((c, D_BK), f32)
    cums = [jnp.dot(ltri, g, precision=_HI, preferred_element_type=f32) for g in gs]
    pmats = []
    for q, k, cum in zip(qs, ks, cums):
        parts = []
        for s in range(c):
            e = jnp.exp(jnp.where(trow >= s, cum - cum[s:s + 1, :], NEG))
            parts.append(q * k[s:s + 1, :] * e)
        pmats.append(_bf(jnp.concatenate(parts, axis=0)))
    rs = [jnp.dot(pm, seg, preferred_element_type=f32) for pm in pmats]
    qes = [q * jnp.exp(cum) for q, cum in zip(qs, cums)]
    kds = [k * jnp.exp(cum[c - 1:c, :] - cum) for k, cum in zip(ks, cums)]
    stb = [_bf(st) for st in sts]
    inters = [[lax.dot_general(_bf(jnp.where(hmask[h], qes[i], zero)), stb[i], _NT,
                               preferred_element_type=f32) for h in heads] for i in range(n)]
    vbs = [_bf(v) for v in vs]
    upds = [[lax.dot_general(vbs[i][:, h * dv:(h + 1) * dv], _bf(jnp.where(hmask[h], kds[i], zero)),
                             _TN, preferred_element_type=f32) for h in heads] for i in range(n)]
    outs, new = [], []
    for i in range(n):
        r, v = rs[i], vs[i]
        o = r[0:c] * v[0:1, :]
        for s in range(1, c):
            o = o + r[s * c:(s + 1) * c] * v[s:s + 1, :]
        outs.append(o + jnp.concatenate(inters[i], axis=1))
        upd = upds[i][0]
        for h in range(1, N_HEADS_B):
            upd = upd + upds[i][h]
        new.append(sts[i] * jnp.exp(cums[i][c - 1:c, :]) + upd)
    return outs, new


def _gla_kernel(chunk, q_ref, k_ref, g_ref, v_ref, s0_ref, seg_ref, o_ref, sout_ref, s_ref):
    tb = pl.program_id(1)
    c = chunk
    nsb = q_ref.shape[0]
    nchunk = q_ref.shape[1] // c

    @pl.when(tb == 0)
    def _():
        for sq in range(nsb):
            s_ref[sq] = s0_ref[sq].T

    rows = lax.broadcasted_iota(jnp.int32, (c, c), 0)
    cols = lax.broadcasted_iota(jnp.int32, (c, c), 1)
    ltri = jnp.where(rows >= cols, 1.0, 0.0)
    trow = lax.broadcasted_iota(jnp.int32, (c, D_BK), 0)
    lane = lax.broadcasted_iota(jnp.int32, (c, D_BK), 1)
    hmask = [(lane >= h * DK_B) & (lane < (h + 1) * DK_B) for h in range(N_HEADS_B)]

    def body(ci, carry):
        t0 = pl.multiple_of(ci * c, c)
        tok = lambda ref: [ref[sq, pl.ds(t0, c), :] for sq in range(nsb)]
        outs, new = _gla_chunks(c, tok(q_ref), tok(k_ref), tok(g_ref), tok(v_ref),
                                [s_ref[sq] for sq in range(nsb)], seg_ref[...], ltri, hmask, trow)
        for sq in range(nsb):
            o_ref[sq, pl.ds(t0, c), :] = outs[sq]
            s_ref[sq] = new[sq]
        return carry

    lax.fori_loop(0, nchunk, body, 0)

    @pl.when(tb == pl.num_programs(1) - 1)
    def _():
        for sq in range(nsb):
            sout_ref[sq] = s_ref[sq].T


def _gla(qb, kb, gk, vb, s0, seg, *, chunk, tblock, nsb):
    nseq, t, _ = qb.shape
    assert t % tblock == 0 and tblock % chunk == 0 and nseq % nsb == 0
    ntb = t // tblock
    tok = lambda w: pl.BlockSpec((nsb, tblock, w), lambda s, tb: (s, tb, 0))
    st_spec = pl.BlockSpec((nsb, D_BK, DV_B), lambda s, tb: (s, 0, 0))
    return pl.pallas_call(
        functools.partial(_gla_kernel, chunk),
        grid=(nseq // nsb, ntb),
        in_specs=[tok(D_BK), tok(D_BK), tok(D_BK), tok(D_BV), st_spec,
                  pl.BlockSpec(seg.shape, lambda s, tb: (0, 0))],
        out_specs=[tok(D_BV), st_spec],
        out_shape=[jax.ShapeDtypeStruct((nseq, t, D_BV), jnp.float32),
                   jax.ShapeDtypeStruct((nseq, D_BK, DV_B), jnp.float32)],
        scratch_shapes=[pltpu.VMEM((nsb, DV_B, D_BK), jnp.float32)],
        compiler_params=pltpu.CompilerParams(dimension_semantics=("arbitrary", "arbitrary"),
                                             vmem_limit_bytes=VMEM_LIMIT),
        name="gla_c%d" % chunk,
    )(qb, kb, gk, vb, s0, seg)


def _post_kernel(x_ref, oa_ref, ob_ref, gb_ref, gn_ref, wo_ref, nm_ref, wu_ref, wd_ref, nf_ref,
                 y_ref, x1_ref, h2_ref, acc_ref):
    kf = pl.program_id(1)

    @pl.when(kf == 0)
    def _():
        ob = ob_ref[...]
        gb = gb_ref[...]
        gn = gn_ref[...]
        heads = []
        for h in range(N_HEADS_B):
            sl = slice(h * DV_B, (h + 1) * DV_B)
            gate = gb[:, sl]
            heads.append(_rms(ob[:, sl], gn[:, sl]) * (gate * (1.0 / (1.0 + jnp.exp(-gate)))))
        mix = _bf(jnp.concatenate([oa_ref[...]] + heads, axis=1))
        x1 = x_ref[...] + jnp.dot(mix, wo_ref[...], preferred_element_type=jnp.float32)
        x1_ref[...] = x1
        h2_ref[...] = _bf(_rms(x1, nm_ref[...]))
        acc_ref[...] = jnp.zeros_like(acc_ref)

    u = jnp.maximum(jnp.dot(h2_ref[...], wu_ref[...], preferred_element_type=jnp.float32), 0.0)
    acc_ref[...] += jnp.dot(_bf(u * u), wd_ref[...], preferred_element_type=jnp.float32)

    @pl.when(kf == pl.num_programs(1) - 1)
    def _():
        y_ref[...] = _rms(x1_ref[...] + acc_ref[...], nf_ref[...])


def _post(x2d, oa, ob, gb, gla_norm, w_o, norm_mlp, w_up, w_down, norm_final):
    n = x2d.shape[0]
    tm = POST_TILE if n % POST_TILE == 0 else TOKEN_TILE
    tf = FF_TILE
    assert n % tm == 0 and D_FF % tf == 0
    row = lambda w: pl.BlockSpec((tm, w), lambda i, kf: (i, 0))
    full = lambda a: pl.BlockSpec(a.shape, lambda i, kf: (0,) * a.ndim)
    return pl.pallas_call(
        _post_kernel,
        grid=(n // tm, D_FF // tf),
        in_specs=[row(D_MODEL), row(D_A), row(D_BV), row(D_BV), full(gla_norm), full(w_o),
                  full(norm_mlp),
                  pl.BlockSpec((D_MODEL, tf), lambda i, kf: (0, kf)),
                  pl.BlockSpec((tf, D_MODEL), lambda i, kf: (kf, 0)),
                  full(norm_final)],
        out_specs=row(D_MODEL),
        out_shape=jax.ShapeDtypeStruct((n, D_MODEL), jnp.float32),
        scratch_shapes=[pltpu.VMEM((tm, D_MODEL), jnp.float32),
                        pltpu.VMEM((tm, D_MODEL), jnp.bfloat16),
                        pltpu.VMEM((tm, D_MODEL), jnp.float32)],
        compiler_params=pltpu.CompilerParams(dimension_semantics=("arbitrary", "arbitrary"),
                                             vmem_limit_bytes=VMEM_LIMIT),
        name="post",
    )(x2d, oa, ob, gb, gla_norm, w_o, norm_mlp, w_up, w_down, norm_final)


def _seg_matrix():
    d = np.arange(D_BK)[:, None] // DK_B
    e = np.arange(D_BV)[None, :] // DV_B
    return jnp.asarray(d == e, jnp.bfloat16)


def _layer(l, x_prompt, x_sample, cache_k, cache_v, state_gla, page_table, norm_mix, w_in, w_gk2,
           b_gk, gla_norm, w_o, norm_mlp, w_up, w_down):
    b, t_p, _ = x_prompt.shape
    db, t_s, _ = x_sample.shape
    page = cache_k.shape[3]
    n_pages = page_table.shape[1]
    past = n_pages * page
    assert past % MOBA_BLOCK == 0 and past // MOBA_BLOCK >= MOBA_TOPK
    assert t_s <= 8 and TOKEN_TILE % t_s == 0

    w_in_p = _bf(jnp.concatenate(
        [w_in[l], jnp.zeros((D_MODEL, D_IN_PAD - D_IN), w_in.dtype)], axis=1))
    w_gk2_p = _bf(jnp.concatenate(
        [w_gk2[l], jnp.zeros((LANES - GATE_RANK, D_BK), w_gk2.dtype)], axis=0))
    nmix = norm_mix[l].reshape(1, D_MODEL)
    bg = b_gk[l].reshape(1, D_BK)
    gn = gla_norm[l].reshape(1, D_BV)
    nmlp = norm_mlp[l].reshape(1, D_MODEL)
    wo, wu, wd = _bf(w_o[l]), _bf(w_up[l]), _bf(w_down[l])
    seg = _seg_matrix()

    xp2 = x_prompt.reshape(b * t_p, D_MODEL)
    tab_p = _rope_tables(jnp.arange(t_p, dtype=jnp.int32))
    (qs, qf, kbf, vtb, kt_p, vt_p, means, qb, kb, vb, gb, gk) = _in_proj(
        xp2, nmix, w_in_p, w_gk2_p, bg, tab_p, sample=False, nseq=b, seq_len=t_p)
    nb = t_p // MOBA_BLOCK
    oa_p = _moba_prompt(qs.reshape(b, t_p, D_A), qf.reshape(b, t_p, D_A), kbf.reshape(b, t_p, D_A),
                        vtb, means.reshape(b, nb, D_A))
    s0_p = jnp.zeros((b, D_BK, DV_B), state_gla.dtype)
    ob_p, s_p = _gla(qb.reshape(b, t_p, D_BK), kb.reshape(b, t_p, D_BK), gk.reshape(b, t_p, D_BK),
                     vb.reshape(b, t_p, D_BV), s0_p, seg, chunk=GLA_CHUNK_PROMPT,
                     tblock=min(GLA_TBLOCK, t_p), nsb=b)
    prompt = (xp2, oa_p.reshape(b * t_p, D_A), ob_p.reshape(b * t_p, D_BV), gb)
    to_rows = lambda a: jnp.swapaxes(a.reshape(b, N_HEADS_A, HEAD_DIM_A, t_p), 2, 3)
    k_p, v_p = to_rows(kt_p), to_rows(vt_p)

    xs2 = x_sample.reshape(db * t_s, D_MODEL)
    pos_s = past + jnp.tile(jnp.arange(t_s, dtype=jnp.int32), db)
    tab_s = _rope_tables(pos_s)
    (q_s, k_s, v_s, qb_s, kb_s, vb_s, gb_s, gk_s) = _in_proj(
        xs2, nmix, w_in_p, w_gk2_p, bg, tab_s, sample=True, nseq=db, seq_len=t_s)
    oa_s = _moba_sample(l, page_table, q_s, k_s, v_s,
                        jnp.swapaxes(cache_k, 3, 4), jnp.swapaxes(cache_v, 3, 4))
    oa_s = oa_s.transpose(0, 2, 1, 3).reshape(db * t_s, D_A)
    nsb = GLA_SEQS_PER_STEP if db % GLA_SEQS_PER_STEP == 0 else 1
    ob_s, s_s = _gla(qb_s.reshape(db, t_s, D_BK), kb_s.reshape(db, t_s, D_BK),
                     gk_s.reshape(db, t_s, D_BK), vb_s.reshape(db, t_s, D_BV),
                     state_gla[l].reshape(db, D_BK, DV_B), seg, chunk=t_s, tblock=t_s, nsb=nsb)
    sample = (xs2, oa_s, ob_s.reshape(db * t_s, D_BV), gb_s)
    return prompt, sample, (gn, wo, nmlp, wu, wd), (k_p, v_p, s_p, k_s, v_s, s_s)


def kernel(x_prompt, x_sample, cache_k, cache_v, state_gla, page_table, norm_mix, w_in, w_gk2, b_gk,
           gla_norm, w_o, norm_mlp, w_up, w_down, norm_final):
    depth = norm_mix.shape[0]
    assert depth == 1, "the trunk is one layer deep"
    b, t_p, _ = x_prompt.shape
    db, t_s, _ = x_sample.shape
    prompt, sample, weights, new = _layer(0, x_prompt, x_sample, cache_k, cache_v, state_gla,
                                          page_table, norm_mix, w_in, w_gk2, b_gk, gla_norm, w_o,
                                          norm_mlp, w_up, w_down)
    nf = norm_final.reshape(1, D_MODEL)
    y_p = _post(*prompt, *weights, nf).reshape(b, t_p, D_MODEL)
    y_s = _post(*sample, *weights, nf).reshape(db, t_s, D_MODEL)
    k_p, v_p, s_p, k_s, v_s, s_s = new
    st = lambda s: s.reshape(1, s.shape[0], N_HEADS_B, DK_B, DV_B)
    return (y_p, y_s, k_p[None], v_p[None], st(s_p), k_s[None], v_s[None], st(s_s))
```

```python
import functools

import jax
import jax.numpy as jnp
import numpy as np
from jax import lax
from jax.experimental import pallas as pl
from jax.experimental.pallas import tpu as pltpu

D_MODEL = 1024
HEAD_DIM_A = 64
D_A = D_MODEL // 2
N_HEADS_A = D_A // HEAD_DIM_A
MOBA_BLOCK = 256
MOBA_TOPK = 3
ROPE_DIM = HEAD_DIM_A // 4
ROPE_THETA = 500000.0
D_BV = D_MODEL // 2
N_HEADS_B = 4
DV_B = D_BV // N_HEADS_B
DK_B = DV_B // 2
D_BK = N_HEADS_B * DK_B
GATE_RANK = 16
GATE_NORMALIZER = 16.0
D_IN = 3 * D_A + 2 * D_BK + 2 * D_BV + GATE_RANK
D_FF = 4 * D_MODEL
EPS = 1e-6
NEG = -1e30
LOG2_E = 1.4426950408889634

LANES = 128
D_IN_PAD = D_IN - GATE_RANK + LANES
_OFF_QA, _OFF_KA, _OFF_VA = 0, D_A, 2 * D_A
_OFF_QB = 3 * D_A
_OFF_KB = _OFF_QB + D_BK
_OFF_VB = _OFF_KB + D_BK
_OFF_GB = _OFF_VB + D_BV
_OFF_LR = _OFF_GB + D_BV

TOKEN_TILE = 256
POST_TILE = 512
GLA_CHUNK_PROMPT = 16
GLA_TBLOCK = 256
GLA_SEQS_PER_STEP = 8
FF_TILE = 1024
SAMPLE_HEAD_GROUP = 2
SAMPLE_SLOTS = 3
VMEM_LIMIT = 48 * 1024 * 1024

_HI = lax.Precision.HIGHEST
_NT = (((1,), (1,)), ((), ()))
_TN = (((0,), (0,)), ((), ()))


def _bf(x):
    return x.astype(jnp.bfloat16)


def _rms(x, g):
    r = lax.rsqrt(jnp.mean(x * x, axis=-1, keepdims=True) + EPS)
    return (x * r) * g


def _rope(z, cos, sin_lo, sin_hi):
    n = z.shape[1]
    cos = jnp.concatenate([cos] * (n // LANES), axis=1)
    sin_lo = jnp.concatenate([sin_lo] * (n // LANES), axis=1)
    sin_hi = jnp.concatenate([sin_hi] * (n // LANES), axis=1)
    half = ROPE_DIM // 2
    up = pltpu.roll(z, n - half, 1)
    dn = pltpu.roll(z, half, 1)
    return z * cos + up * sin_lo + dn * sin_hi


def _in_proj_kernel(sample, x_ref, g_ref, w_ref, wg_ref, bg_ref, cos_ref, slo_ref, shi_ref, *outs):
    tm = x_ref.shape[0]
    hb = _bf(_rms(x_ref[...], g_ref[...]))

    def proj(off, width):
        return jnp.dot(hb, w_ref[:, off:off + width], preferred_element_type=jnp.float32)

    cos, slo, shi = cos_ref[...], slo_ref[...], shi_ref[...]
    q = _rope(proj(_OFF_QA, D_A), cos, slo, shi)
    k = _rope(proj(_OFF_KA, D_A), cos, slo, shi)
    v = proj(_OFF_VA, D_A)
    if sample:
        (q_hm, k_hm, v_hm, qb_o, kb_o, vb_o, gb_o, gk_o) = outs
        nseq = q_hm.shape[0]
        for h in range(N_HEADS_A):
            sl = slice(h * HEAD_DIM_A, (h + 1) * HEAD_DIM_A)
            q_hm[:, h] = q[:, sl].reshape(nseq, tm // nseq, HEAD_DIM_A)
            k_hm[:, h] = k[:, sl].reshape(nseq, tm // nseq, HEAD_DIM_A)
            v_hm[:, h] = v[:, sl].reshape(nseq, tm // nseq, HEAD_DIM_A)
    else:
        (qs_o, qf_o, kbf_o, vtb_o, kt_o, vt_o, mean_o, qb_o, kb_o, vb_o, gb_o, gk_o) = outs
        qs_o[...] = _bf(q * (HEAD_DIM_A ** -0.5 * LOG2_E))
        qf_o[...] = q
        kbf_o[...] = _bf(k)
        vt = v.T
        kt_o[0] = k.T
        vt_o[0] = vt
        vtb_o[0, 0] = _bf(vt)
        mean_o[0, 0] = jnp.sum(k, axis=0, keepdims=True) * (1.0 / tm)
    qb_o[...] = proj(_OFF_QB, D_BK) * (DK_B ** -0.5)
    kb_o[...] = proj(_OFF_KB, D_BK)
    vb_o[...] = proj(_OFF_VB, D_BV)
    gb_o[...] = proj(_OFF_GB, D_BV)
    lr = _bf(proj(_OFF_LR, LANES))
    zg = jnp.dot(lr, wg_ref[...], preferred_element_type=jnp.float32) + bg_ref[...]
    log_sig = jnp.minimum(zg, 0.0) - jnp.log1p(jnp.exp(-jnp.abs(zg)))
    gk_o[...] = log_sig * (1.0 / GATE_NORMALIZER)


def _in_proj(x2d, norm_mix, w_in_p, w_gk2_p, b_gk, tables, *, sample, nseq, seq_len):
    n = x2d.shape[0]
    tm = TOKEN_TILE
    assert n % tm == 0
    nt = n // tm
    cos, slo, shi = tables
    f32 = jnp.float32
    row = lambda w: pl.BlockSpec((tm, w), lambda i: (i, 0))
    full = lambda a: pl.BlockSpec(a.shape, lambda i: (0,) * a.ndim)
    gla_shapes = [jax.ShapeDtypeStruct((n, D_BK), f32), jax.ShapeDtypeStruct((n, D_BK), f32),
                  jax.ShapeDtypeStruct((n, D_BV), f32), jax.ShapeDtypeStruct((n, D_BV), f32),
                  jax.ShapeDtypeStruct((n, D_BK), f32)]
    gla_specs = [row(D_BK), row(D_BK), row(D_BV), row(D_BV), row(D_BK)]
    if sample:
        assert tm % seq_len == 0
        spt = tm // seq_len
        hm = jax.ShapeDtypeStruct((nseq, N_HEADS_A, seq_len, HEAD_DIM_A), f32)
        hm_spec = pl.BlockSpec((spt, N_HEADS_A, seq_len, HEAD_DIM_A), lambda i: (i, 0, 0, 0))
        out_shape = [hm, hm, hm] + gla_shapes
        out_specs = [hm_spec, hm_spec, hm_spec] + gla_specs
        tab_spec = row(LANES)
    else:
        assert seq_len % tm == 0 and tm == MOBA_BLOCK
        tps = seq_len // tm
        nb = tps
        chan_major = jax.ShapeDtypeStruct((nseq, D_A, seq_len), f32)
        chan_spec = pl.BlockSpec((1, D_A, tm), lambda i: (i // tps, 0, i % tps))
        out_shape = [jax.ShapeDtypeStruct((n, D_A), jnp.bfloat16),
                     jax.ShapeDtypeStruct((n, D_A), f32),
                     jax.ShapeDtypeStruct((n, D_A), jnp.bfloat16),
                     jax.ShapeDtypeStruct((nseq, nb, D_A, tm), jnp.bfloat16),
                     chan_major, chan_major,
                     jax.ShapeDtypeStruct((nseq, nb, 1, D_A), f32)] + gla_shapes
        out_specs = [row(D_A), row(D_A), row(D_A),
                     pl.BlockSpec((1, 1, D_A, tm), lambda i: (i // tps, i % tps, 0, 0)),
                     chan_spec, chan_spec,
                     pl.BlockSpec((1, 1, 1, D_A), lambda i: (i // tps, i % tps, 0, 0))] + gla_specs
        tab_spec = pl.BlockSpec((tm, LANES), lambda i: (i % tps, 0))
    return pl.pallas_call(
        functools.partial(_in_proj_kernel, sample),
        grid=(nt,),
        in_specs=[row(D_MODEL), full(norm_mix), full(w_in_p), full(w_gk2_p), full(b_gk),
                  tab_spec, tab_spec, tab_spec],
        out_specs=out_specs,
        out_shape=out_shape,
        compiler_params=pltpu.CompilerParams(dimension_semantics=("arbitrary",),
                                             vmem_limit_bytes=VMEM_LIMIT),
        name="in_proj_sample" if sample else "in_proj_prompt",
    )(x2d, norm_mix, w_in_p, w_gk2_p, b_gk, cos, slo, shi)


def _rope_tables(pos):
    half = ROPE_DIM // 2
    inv = ROPE_THETA ** (-jnp.arange(half, dtype=jnp.float32) / half)
    ang = pos.astype(jnp.float32)[:, None] * inv[None, :]
    c, s = jnp.cos(ang), jnp.sin(ang)
    n = pos.shape[0]
    ones = jnp.ones((n, HEAD_DIM_A - ROPE_DIM), jnp.float32)
    zeros = jnp.zeros((n, HEAD_DIM_A - ROPE_DIM), jnp.float32)
    zh = jnp.zeros((n, half), jnp.float32)
    cos = jnp.concatenate([c, c, ones], axis=1)
    sin_lo = jnp.concatenate([-s, zh, zeros], axis=1)
    sin_hi = jnp.concatenate([zh, s, zeros], axis=1)
    rep = LANES // HEAD_DIM_A
    return tuple(jnp.concatenate([t] * rep, axis=1) for t in (cos, sin_lo, sin_hi))


def _rank_rows(s, blk, nb):
    rank = jnp.zeros(s.shape, jnp.float32)
    for jp in range(nb):
        row = s[jp:jp + 1, :]
        before = (row > s) | ((row == s) & (blk > jp))
        rank = rank + jnp.where(before, 1.0, 0.0)
    return rank


def _moba_kernel(qs_ref, qf_ref, k_ref, vt_ref, mean_ref, oh_ref, o_ref, qm_ref, sel_ref, m_ref,
                 l_ref, acc_ref, s_ref, p_ref):
    i = pl.program_id(1)
    bq = qs_ref.shape[1]
    nb = mean_ref.shape[1]
    hd = HEAD_DIM_A
    nh = N_HEADS_A
    lane = lax.broadcasted_iota(jnp.int32, (bq, LANES), 1)
    halves = (lane < hd, lane >= hd)
    blk = lax.broadcasted_iota(jnp.int32, (nb, bq), 0)
    pair = lambda h: slice((h // 2) * LANES, (h // 2 + 1) * LANES)

    row0 = pl.multiple_of(i * bq, bq)
    for h in range(nh):
        q2 = qs_ref[0, :, pair(h)]
        qm_ref[h, :, 0:LANES] = jnp.where(halves[h % 2], q2, jnp.zeros_like(q2))
    for h in range(nh):
        kd = k_ref[0, pl.ds(row0, bq), pair(h)]
        s_ref[h] = lax.dot_general(kd, qm_ref[h, :, 0:LANES], _NT,
                                   preferred_element_type=jnp.float32)
    for h in range(nh):
        qfh = jnp.where(halves[h % 2], qf_ref[0, :, pair(h)], 0.0)
        sel_ref[h] = lax.dot_general(mean_ref[0, :, pair(h)], qfh, _NT, precision=_HI,
                                     preferred_element_type=jnp.float32)
    for h in range(nh):
        s = jnp.where(blk < i, sel_ref[h], NEG)
        rank = _rank_rows(s, blk, nb)
        bias = jnp.where((rank < MOBA_TOPK) & (blk < i), 0.0, NEG)
        bias = jnp.concatenate([bias, jnp.zeros((LANES - nb, bq), jnp.float32)], axis=0)
        qm_ref[h, :, LANES:2 * LANES] = _bf(bias.T)

    kidx = lax.broadcasted_iota(jnp.int32, (bq, bq), 0)
    qidx = lax.broadcasted_iota(jnp.int32, (bq, bq), 1)
    for h in range(nh):
        s = jnp.where(kidx <= qidx, s_ref[h], NEG)
        m = jnp.max(s, axis=0, keepdims=True)
        p = jnp.exp2(s - m)
        m_ref[h] = m
        l_ref[h] = jnp.sum(p, axis=0, keepdims=True)
        p_ref[h] = _bf(p)
    for h in range(nh):
        acc_ref[h] = jnp.dot(vt_ref[0, i, h * hd:(h + 1) * hd, :], p_ref[h],
                             preferred_element_type=jnp.float32)

    def past(j, carry):
        rowj = pl.multiple_of(j * bq, bq)
        onehot = oh_ref[j]
        for h in range(nh):
            kj = jnp.concatenate([k_ref[0, pl.ds(rowj, bq), pair(h)], onehot], axis=1)
            s_ref[h] = lax.dot_general(kj, qm_ref[h], _NT, preferred_element_type=jnp.float32)
        alphas = []
        for h in range(nh):
            m_old = m_ref[h]
            m_new = jnp.maximum(m_old, jnp.max(s_ref[h], axis=0, keepdims=True))
            alpha = jnp.exp2(m_old - m_new)
            p = jnp.exp2(s_ref[h] - m_new)
            m_ref[h] = m_new
            l_ref[h] = alpha * l_ref[h] + jnp.sum(p, axis=0, keepdims=True)
            p_ref[h] = _bf(p)
            alphas.append(alpha)
        for h in range(nh):
            acc_ref[h] = alphas[h] * acc_ref[h] + jnp.dot(
                vt_ref[0, j, h * hd:(h + 1) * hd, :], p_ref[h], preferred_element_type=jnp.float32)
        return carry

    lax.fori_loop(0, i, past, 0)
    for p2 in range(nh // 2):
        ot = jnp.concatenate([acc_ref[2 * p2] / l_ref[2 * p2],
                              acc_ref[2 * p2 + 1] / l_ref[2 * p2 + 1]], axis=0)
        o_ref[0, :, p2 * LANES:(p2 + 1) * LANES] = ot.T


def _moba_prompt(qs, qf, kbf, vtb, means):
    b, t, _ = qs.shape
    nb = t // MOBA_BLOCK
    bq = MOBA_BLOCK
    assert nb <= LANES
    onehot = jnp.asarray(np.arange(LANES)[None, None, :] == np.arange(nb)[:, None, None],
                         jnp.bfloat16) * jnp.ones((nb, bq, LANES), jnp.bfloat16)
    return pl.pallas_call(
        _moba_kernel,
        grid=(b, nb),
        in_specs=[pl.BlockSpec((1, bq, D_A), lambda b_, i: (b_, i, 0)),
                  pl.BlockSpec((1, bq, D_A), lambda b_, i: (b_, i, 0)),
                  pl.BlockSpec((1, t, D_A), lambda b_, i: (b_, 0, 0)),
                  pl.BlockSpec((1, nb, D_A, bq), lambda b_, i: (b_, 0, 0, 0)),
                  pl.BlockSpec((1, nb, D_A), lambda b_, i: (b_, 0, 0)),
                  pl.BlockSpec((nb, bq, LANES), lambda b_, i: (0, 0, 0))],
        out_specs=pl.BlockSpec((1, bq, D_A), lambda b_, i: (b_, i, 0)),
        out_shape=jax.ShapeDtypeStruct((b, t, D_A), jnp.float32),
        scratch_shapes=[pltpu.VMEM((N_HEADS_A, bq, 2 * LANES), jnp.bfloat16),
                        pltpu.VMEM((N_HEADS_A, nb, bq), jnp.float32),
                        pltpu.VMEM((N_HEADS_A, 1, bq), jnp.float32),
                        pltpu.VMEM((N_HEADS_A, 1, bq), jnp.float32),
                        pltpu.VMEM((N_HEADS_A, HEAD_DIM_A, bq), jnp.float32),
                        pltpu.VMEM((N_HEADS_A, bq, bq), jnp.float32),
                        pltpu.VMEM((N_HEADS_A, bq, bq), jnp.bfloat16)],
        compiler_params=pltpu.CompilerParams(dimension_semantics=("arbitrary", "arbitrary"),
                                             vmem_limit_bytes=VMEM_LIMIT),
        name="moba_prompt",
    )(qs, qf, kbf, vtb, means, onehot)


def _pages_copy(cache_ref, layer, page, head0, buf, slot, pg, sem):
    hg = buf.shape[1]
    return pltpu.make_async_copy(cache_ref.at[layer, page, pl.ds(head0, hg)],
                                 buf.at[slot, :, :, pl.ds(pg * LANES, LANES)], sem)


def _moba_sample_kernel(layer, nseq, n_pages, pt_ref, q_ref, kn_ref, vn_ref, ck_ref, cv_ref, o_ref,
                        kbuf, vbuf, sem, s_ref, p_ref):
    b, g = pl.program_id(0), pl.program_id(1)
    ngrp = pl.num_programs(1)
    hg, hd = kbuf.shape[1], kbuf.shape[2]
    nkey = kbuf.shape[3]
    nblk = nkey // MOBA_BLOCK
    ppb = MOBA_BLOCK // LANES
    step = b * ngrp + g
    last = nseq * ngrp - 1
    nslot = kbuf.shape[0]
    ahead = nslot - 1
    slot = step % nslot

    def issue(bb, gg, sl, pg):
        page = pt_ref[bb * n_pages + pg]
        _pages_copy(ck_ref, layer, page, gg * hg, kbuf, sl, pg, sem.at[0, sl]).start()
        _pages_copy(cv_ref, layer, page, gg * hg, vbuf, sl, pg, sem.at[1, sl]).start()

    def wait_all(sl):
        for pg in range(n_pages):
            _pages_copy(ck_ref, layer, 0, 0, kbuf, sl, pg, sem.at[0, sl]).wait()
        for pg in range(n_pages):
            _pages_copy(cv_ref, layer, 0, 0, vbuf, sl, pg, sem.at[1, sl]).wait()

    @pl.when(step == 0)
    def _():
        for d in range(ahead):
            first = jnp.minimum(d, last)
            for pg in range(n_pages):
                issue(first // ngrp, first % ngrp, d, pg)

    wait_all(slot)
    nxt = jnp.minimum(step + ahead, last)
    nxt_b, nxt_g = nxt // ngrp, nxt % ngrp
    nxt_slot = (step + ahead) % nslot

    ts = q_ref.shape[2]
    lane = lax.broadcasted_iota(jnp.int32, (ts, LANES), 1)
    trow = lax.broadcasted_iota(jnp.int32, (ts, LANES), 0)
    mlane = lax.broadcasted_iota(jnp.int32, (hd, LANES), 1)
    pad = jnp.zeros((LANES - ts, hd), jnp.float32)
    heads = range(hg)
    mts = []
    for h in heads:
        mt = jnp.zeros((hd, LANES), jnp.float32)
        for j in range(nblk):
            x = kbuf[slot, h, :, j * MOBA_BLOCK:j * MOBA_BLOCK + LANES]
            for pp in range(1, ppb):
                x = x + kbuf[slot, h, :, j * MOBA_BLOCK + pp * LANES:j * MOBA_BLOCK + (pp + 1) * LANES]
            mt = jnp.where(mlane == j, jnp.sum(x, axis=1, keepdims=True), mt)
            if h == 0:
                for pp in range(ppb):
                    issue(nxt_b, nxt_g, nxt_slot, j * ppb + pp)
        mts.append(mt * (1.0 / MOBA_BLOCK))
    qs = [q_ref[0, h] for h in heads]
    sbs = [jnp.where(lane < nblk,
                     jnp.dot(qs[h], mts[h], precision=_HI, preferred_element_type=jnp.float32), NEG)
           for h in heads]
    qbs = [_bf(q * (hd ** -0.5)) for q in qs]
    for h in heads:
        s_ref[h] = jnp.dot(qbs[h], _bf(kbuf[slot, h]), preferred_element_type=jnp.float32)
    sels = []
    for h in heads:
        rank = jnp.zeros((ts, LANES), jnp.float32)
        for jp in range(nblk):
            col = sbs[h][:, jp:jp + 1]
            before = (col > sbs[h]) | ((col == sbs[h]) & (lane > jp))
            rank = rank + jnp.where(before, 1.0, 0.0)
        sels.append(jnp.where((rank < MOBA_TOPK) & (lane < nblk), 1.0, 0.0))
    vns, pos, ls = [], [], []
    for h in heads:
        s = jnp.concatenate(
            [jnp.where(sels[h][:, j:j + 1] > 0.0, s_ref[h, :, j * MOBA_BLOCK:(j + 1) * MOBA_BLOCK],
                       NEG) for j in range(nblk)], axis=1)
        kn = _bf(jnp.concatenate([kn_ref[0, h], pad], axis=0))
        vns.append(_bf(jnp.concatenate([vn_ref[0, h], pad], axis=0)))
        so = lax.dot_general(qbs[h], kn, _NT, preferred_element_type=jnp.float32)
        so = jnp.where(lane <= trow, so, NEG)
        m = jnp.maximum(jnp.max(s, axis=1, keepdims=True), jnp.max(so, axis=1, keepdims=True))
        p = jnp.exp(s - m)
        po = jnp.exp(so - m)
        ls.append(jnp.sum(p, axis=1, keepdims=True) + jnp.sum(po, axis=1, keepdims=True))
        p_ref[h] = _bf(p)
        pos.append(_bf(po))
    for h in heads:
        o = (lax.dot_general(p_ref[h], _bf(vbuf[slot, h]), _NT, preferred_element_type=jnp.float32)
             + jnp.dot(pos[h], vns[h], preferred_element_type=jnp.float32))
        o_ref[0, h] = o / ls[h]

    @pl.when(step == last)
    def _():
        for d in range(1, nslot):
            wait_all((step + d) % nslot)


def _moba_sample(layer, page_table, q_hm, k_hm, v_hm, cache_kt, cache_vt):
    db, nh, ts, hd = q_hm.shape
    n_pages = page_table.shape[1]
    page = cache_kt.shape[4]
    hg = SAMPLE_HEAD_GROUP
    assert page == LANES and MOBA_BLOCK % page == 0 and nh % hg == 0
    assert (n_pages * page) % MOBA_BLOCK == 0 and n_pages * page // MOBA_BLOCK <= LANES
    blk = pl.BlockSpec((1, hg, ts, hd), lambda b, g, pt: (b, g, 0, 0))
    grid_spec = pltpu.PrefetchScalarGridSpec(
        num_scalar_prefetch=1,
        grid=(db, nh // hg),
        in_specs=[blk, blk, blk, pl.BlockSpec(memory_space=pl.ANY), pl.BlockSpec(memory_space=pl.ANY)],
        out_specs=blk,
        scratch_shapes=[pltpu.VMEM((SAMPLE_SLOTS, hg, hd, n_pages * page), jnp.float32),
                        pltpu.VMEM((SAMPLE_SLOTS, hg, hd, n_pages * page), jnp.float32),
                        pltpu.SemaphoreType.DMA((2, SAMPLE_SLOTS)),
                        pltpu.VMEM((hg, ts, n_pages * page), jnp.float32),
                        pltpu.VMEM((hg, ts, n_pages * page), jnp.bfloat16)])
    return pl.pallas_call(
        functools.partial(_moba_sample_kernel, layer, db, n_pages),
        grid_spec=grid_spec,
        out_shape=jax.ShapeDtypeStruct((db, nh, ts, hd), jnp.float32),
        compiler_params=pltpu.CompilerParams(dimension_semantics=("arbitrary", "arbitrary"),
                                             vmem_limit_bytes=VMEM_LIMIT),
        name="moba_sample",
    )(page_table.reshape(-1), q_hm, k_hm, v_hm, cache_kt, cache_vt)


def _gla_chunks(c, qs, ks, gs, vs, sts, seg, ltri, hmask, trow):
    dk, dv = DK_B, DV_B
    n = len(qs)
    heads = range(N_HEADS_B)
    f32 = jnp.float32
    zero = jnp.zeros((c, D_BK), f32)
    cums = [jnp.dot(ltri, g, precision=_HI, preferred_element_type=f32) for g in gs]
    pmats = []
    for q, k, cum in zip(qs, ks, cums):
        parts = []
        for s in range(c):
            e = jnp.exp(jnp.where(trow >= s, cum - cum[s:s + 1, :], NEG))
            parts.append(q * k[s:s + 1, :] * e)
        pmats.append(_bf(jnp.concatenate(parts, axis=0)))
    rs = [jnp.dot(pm, seg, preferred_element_type=f32) for pm in pmats]
    qes = [q * jnp.exp(cum) for q, cum in zip(qs, cums)]
    kds = [k * jnp.exp(cum[c - 1:c, :] - cum) for k, cum in zip(ks, cums)]
    stb = [_bf(st) for st in sts]
    inters = [[lax.dot_general(_bf(jnp.where(hmask[h], qes[i], zero)), stb[i], _NT,
                               preferred_element_type=f32) for h in heads] for i in range(n)]
    vbs = [_bf(v) for v in vs]
    upds = [[lax.dot_general(vbs[i][:, h * dv:(h + 1) * dv], _bf(jnp.where(hmask[h], kds[i], zero)),
                             _TN, preferred_element_type=f32) for h in heads] for i in range(n)]
    outs, new = [], []
    for i in range(n):
        r, v = rs[i], vs[i]
        o = r[0:c] * v[0:1, :]
        for s in range(1, c):
            o = o + r[s * c:(s + 1) * c] * v[s:s + 1, :]
        outs.append(o + jnp.concatenate(inters[i], axis=1))
        upd = upds[i][0]
        for h in range(1, N_HEADS_B):
            upd = upd + upds[i][h]
        new.append(sts[i] * jnp.exp(cums[i][c - 1:c, :]) + upd)
    return outs, new


def _gla_kernel(chunk, q_ref, k_ref, g_ref, v_ref, s0_ref, seg_ref, o_ref, sout_ref, s_ref):
    tb = pl.program_id(1)
    c = chunk
    nsb = q_ref.shape[0]
    nchunk = q_ref.shape[1] // c

    @pl.when(tb == 0)
    def _():
        for sq in range(nsb):
            s_ref[sq] = s0_ref[sq].T

    rows = lax.broadcasted_iota(jnp.int32, (c, c), 0)
    cols = lax.broadcasted_iota(jnp.int32, (c, c), 1)
    ltri = jnp.where(rows >= cols, 1.0, 0.0)
    trow = lax.broadcasted_iota(jnp.int32, (c, D_BK), 0)
    lane = lax.broadcasted_iota(jnp.int32, (c, D_BK), 1)
    hmask = [(lane >= h * DK_B) & (lane < (h + 1) * DK_B) for h in range(N_HEADS_B)]

    def body(ci, carry):
        t0 = pl.multiple_of(ci * c, c)
        tok = lambda ref: [ref[sq, pl.ds(t0, c), :] for sq in range(nsb)]
        outs, new = _gla_chunks(c, tok(q_ref), tok(k_ref), tok(g_ref), tok(v_ref),
                                [s_ref[sq] for sq in range(nsb)], seg_ref[...], ltri, hmask, trow)
        for sq in range(nsb):
            o_ref[sq, pl.ds(t0, c), :] = outs[sq]
            s_ref[sq] = new[sq]
        return carry

    lax.fori_loop(0, nchunk, body, 0)

    @pl.when(tb == pl.num_programs(1) - 1)
    def _():
        for sq in range(nsb):
            sout_ref[sq] = s_ref[sq].T


def _gla(qb, kb, gk, vb, s0, seg, *, chunk, tblock, nsb):
    nseq, t, _ = qb.shape
    assert t % tblock == 0 and tblock % chunk == 0 and nseq % nsb == 0
    ntb = t // tblock
    tok = lambda w: pl.BlockSpec((nsb, tblock, w), lambda s, tb: (s, tb, 0))
    st_spec = pl.BlockSpec((nsb, D_BK, DV_B), lambda s, tb: (s, 0, 0))
    return pl.pallas_call(
        functools.partial(_gla_kernel, chunk),
        grid=(nseq // nsb, ntb),
        in_specs=[tok(D_BK), tok(D_BK), tok(D_BK), tok(D_BV), st_spec,
                  pl.BlockSpec(seg.shape, lambda s, tb: (0, 0))],
        out_specs=[tok(D_BV), st_spec],
        out_shape=[jax.ShapeDtypeStruct((nseq, t, D_BV), jnp.float32),
                   jax.ShapeDtypeStruct((nseq, D_BK, DV_B), jnp.float32)],
        scratch_shapes=[pltpu.VMEM((nsb, DV_B, D_BK), jnp.float32)],
        compiler_params=pltpu.CompilerParams(dimension_semantics=("arbitrary", "arbitrary"),
                                             vmem_limit_bytes=VMEM_LIMIT),
        name="gla_c%d" % chunk,
    )(qb, kb, gk, vb, s0, seg)


def _post_kernel(x_ref, oa_ref, ob_ref, gb_ref, gn_ref, wo_ref, nm_ref, wu_ref, wd_ref, nf_ref,
                 y_ref, x1_ref, h2_ref, acc_ref):
    kf = pl.program_id(1)

    @pl.when(kf == 0)
    def _():
        ob = ob_ref[...]
        gb = gb_ref[...]
        gn = gn_ref[...]
        heads = []
        for h in range(N_HEADS_B):
            sl = slice(h * DV_B, (h + 1) * DV_B)
            gate = gb[:, sl]
            heads.append(_rms(ob[:, sl], gn[:, sl]) * (gate * (1.0 / (1.0 + jnp.exp(-gate)))))
        mix = _bf(jnp.concatenate([oa_ref[...]] + heads, axis=1))
        x1 = x_ref[...] + jnp.dot(mix, wo_ref[...], preferred_element_type=jnp.float32)
        x1_ref[...] = x1
        h2_ref[...] = _bf(_rms(x1, nm_ref[...]))
        acc_ref[...] = jnp.zeros_like(acc_ref)

    u = jnp.maximum(jnp.dot(h2_ref[...], wu_ref[...], preferred_element_type=jnp.float32), 0.0)
    acc_ref[...] += jnp.dot(_bf(u * u), wd_ref[...], preferred_element_type=jnp.float32)

    @pl.when(kf == pl.num_programs(1) - 1)
    def _():
        y_ref[...] = _rms(x1_ref[...] + acc_ref[...], nf_ref[...])


def _post(x2d, oa, ob, gb, gla_norm, w_o, norm_mlp, w_up, w_down, norm_final):
    n = x2d.shape[0]
    tm = POST_TILE if n % POST_TILE == 0 else TOKEN_TILE
    tf = FF_TILE
    assert n % tm == 0 and D_FF % tf == 0
    row = lambda w: pl.BlockSpec((tm, w), lambda i, kf: (i, 0))
    full = lambda a: pl.BlockSpec(a.shape, lambda i, kf: (0,) * a.ndim)
    return pl.pallas_call(
        _post_kernel,
        grid=(n // tm, D_FF // tf),
        in_specs=[row(D_MODEL), row(D_A), row(D_BV), row(D_BV), full(gla_norm), full(w_o),
                  full(norm_mlp),
                  pl.BlockSpec((D_MODEL, tf), lambda i, kf: (0, kf)),
                  pl.BlockSpec((tf, D_MODEL), lambda i, kf: (kf, 0)),
                  full(norm_final)],
        out_specs=row(D_MODEL),
        out_shape=jax.ShapeDtypeStruct((n, D_MODEL), jnp.float32),
        scratch_shapes=[pltpu.VMEM((tm, D_MODEL), jnp.float32),
                        pltpu.VMEM((tm, D_MODEL), jnp.bfloat16),
                        pltpu.VMEM((tm, D_MODEL), jnp.float32)],
        compiler_params=pltpu.CompilerParams(dimension_semantics=("arbitrary", "arbitrary"),
                                             vmem_limit_bytes=VMEM_LIMIT),
        name="post",
    )(x2d, oa, ob, gb, gla_norm, w_o, norm_mlp, w_up, w_down, norm_final)


def _seg_matrix():
    d = np.arange(D_BK)[:, None] // DK_B
    e = np.arange(D_BV)[None, :] // DV_B
    return jnp.asarray(d == e, jnp.bfloat16)


def _layer(l, x_prompt, x_sample, cache_k, cache_v, state_gla, page_table, norm_mix, w_in, w_gk2,
           b_gk, gla_norm, w_o, norm_mlp, w_up, w_down):
    b, t_p, _ = x_prompt.shape
    db, t_s, _ = x_sample.shape
    page = cache_k.shape[3]
    n_pages = page_table.shape[1]
    past = n_pages * page
    assert past % MOBA_BLOCK == 0 and past // MOBA_BLOCK >= MOBA_TOPK
    assert t_s <= 8 and TOKEN_TILE % t_s == 0

    w_in_p = _bf(jnp.concatenate(
        [w_in[l], jnp.zeros((D_MODEL, D_IN_PAD - D_IN), w_in.dtype)], axis=1))
    w_gk2_p = _bf(jnp.concatenate(
        [w_gk2[l], jnp.zeros((LANES - GATE_RANK, D_BK), w_gk2.dtype)], axis=0))
    nmix = norm_mix[l].reshape(1, D_MODEL)
    bg = b_gk[l].reshape(1, D_BK)
    gn = gla_norm[l].reshape(1, D_BV)
    nmlp = norm_mlp[l].reshape(1, D_MODEL)
    wo, wu, wd = _bf(w_o[l]), _bf(w_up[l]), _bf(w_down[l])
    seg = _seg_matrix()

    xp2 = x_prompt.reshape(b * t_p, D_MODEL)
    tab_p = _rope_tables(jnp.arange(t_p, dtype=jnp.int32))
    (qs, qf, kbf, vtb, kt_p, vt_p, means, qb, kb, vb, gb, gk) = _in_proj(
        xp2, nmix, w_in_p, w_gk2_p, bg, tab_p, sample=False, nseq=b, seq_len=t_p)
    nb = t_p // MOBA_BLOCK
    oa_p = _moba_prompt(qs.reshape(b, t_p, D_A), qf.reshape(b, t_p, D_A), kbf.reshape(b, t_p, D_A),
                        vtb, means.reshape(b, nb, D_A))
    s0_p = jnp.zeros((b, D_BK, DV_B), state_gla.dtype)
    ob_p, s_p = _gla(qb.reshape(b, t_p, D_BK), kb.reshape(b, t_p, D_BK), gk.reshape(b, t_p, D_BK),
                     vb.reshape(b, t_p, D_BV), s0_p, seg, chunk=GLA_CHUNK_PROMPT,
                     tblock=min(GLA_TBLOCK, t_p), nsb=b)
    prompt = (xp2, oa_p.reshape(b * t_p, D_A), ob_p.reshape(b * t_p, D_BV), gb)
    to_rows = lambda a: jnp.swapaxes(a.reshape(b, N_HEADS_A, HEAD_DIM_A, t_p), 2, 3)
    k_p, v_p = to_rows(kt_p), to_rows(vt_p)

    xs2 = x_sample.reshape(db * t_s, D_MODEL)
    pos_s = past + jnp.tile(jnp.arange(t_s, dtype=jnp.int32), db)
    tab_s = _rope_tables(pos_s)
    (q_s, k_s, v_s, qb_s, kb_s, vb_s, gb_s, gk_s) = _in_proj(
        xs2, nmix, w_in_p, w_gk2_p, bg, tab_s, sample=True, nseq=db, seq_len=t_s)
    oa_s = _moba_sample(l, page_table, q_s, k_s, v_s,
                        jnp.swapaxes(cache_k, 3, 4), jnp.swapaxes(cache_v, 3, 4))
    oa_s = oa_s.transpose(0, 2, 1, 3).reshape(db * t_s, D_A)
    nsb = GLA_SEQS_PER_STEP if db % GLA_SEQS_PER_STEP == 0 else 1
    ob_s, s_s = _gla(qb_s.reshape(db, t_s, D_BK), kb_s.reshape(db, t_s, D_BK),
                     gk_s.reshape(db, t_s, D_BK), vb_s.reshape(db, t_s, D_BV),
                     state_gla[l].reshape(db, D_BK, DV_B), seg, chunk=t_s, tblock=t_s, nsb=nsb)
    sample = (xs2, oa_s, ob_s.reshape(db * t_s, D_BV), gb_s)
    return prompt, sample, (gn, wo, nmlp, wu, wd), (k_p, v_p, s_p, k_s, v_s, s_s)


def kernel(x_prompt, x_sample, cache_k, cache_v, state_gla, page_table, norm_mix, w_in, w_gk2, b_gk,
           gla_norm, w_o, norm_mlp, w_up, w_down, norm_final):
    depth = norm_mix.shape[0]
    assert depth == 1, "the trunk is one layer deep"
    b, t_p, _ = x_prompt.shape
    db, t_s, _ = x_sample.shape
    prompt, sample, weights, new = _layer(0, x_prompt, x_sample, cache_k, cache_v, state_gla,
                                          page_table, norm_mix, w_in, w_gk2, b_gk, gla_norm, w_o,
                                          norm_mlp, w_up, w_down)
    nf = norm_final.reshape(1, D_MODEL)
    y_p = _post(*prompt, *weights, nf).reshape(b, t_p, D_MODEL)
    y_s = _post(*sample, *weights, nf).reshape(db, t_s, D_MODEL)
    k_p, v_p, s_p, k_s, v_s, s_s = new
    st = lambda s: s.reshape(1, s.shape[0], N_HEADS_B, DK_B, DV_B)
    return (y_p, y_s, k_p[None], v_p[None], st(s_p), k_s[None], v_s[None], st(s_s))
```

```python
import functools

import jax
import jax.numpy as jnp
import numpy as np
from jax import lax
from jax.experimental import pallas as pl
from jax.experimental.pallas import tpu as pltpu

D_MODEL = 1024
HEAD_DIM_A = 64
D_A = D_MODEL // 2
N_HEADS_A = D_A // HEAD_DIM_A
MOBA_BLOCK = 256
MOBA_TOPK = 3
ROPE_DIM = HEAD_DIM_A // 4
ROPE_THETA = 500000.0
D_BV = D_MODEL // 2
N_HEADS_B = 4
DV_B = D_BV // N_HEADS_B
DK_B = DV_B // 2
D_BK = N_HEADS_B * DK_B
GATE_RANK = 16
GATE_NORMALIZER = 16.0
D_IN = 3 * D_A + 2 * D_BK + 2 * D_BV + GATE_RANK
D_FF = 4 * D_MODEL
EPS = 1e-6
NEG = -1e30
LOG2_E = 1.4426950408889634

LANES = 128
D_IN_PAD = D_IN - GATE_RANK + LANES
_OFF_QA, _OFF_KA, _OFF_VA = 0, D_A, 2 * D_A
_OFF_QB = 3 * D_A
_OFF_KB = _OFF_QB + D_BK
_OFF_VB = _OFF_KB + D_BK
_OFF_GB = _OFF_VB + D_BV
_OFF_LR = _OFF_GB + D_BV

TOKEN_TILE = 256
POST_TILE = 512
GLA_CHUNK_PROMPT = 16
GLA_TBLOCK = 256
GLA_SEQS_PER_STEP = 8
FF_TILE = 1024
SAMPLE_HEAD_GROUP = 2
SAMPLE_SLOTS = 3
VMEM_LIMIT = 48 * 1024 * 1024

_HI = lax.Precision.HIGHEST
_NT = (((1,), (1,)), ((), ()))
_TN = (((0,), (0,)), ((), ()))


def _bf(x):
    return x.astype(jnp.bfloat16)


def _rms(x, g):
    r = lax.rsqrt(jnp.mean(x * x, axis=-1, keepdims=True) + EPS)
    return (x * r) * g


def _rope(z, cos, sin_lo, sin_hi):
    n = z.shape[1]
    cos = jnp.concatenate([cos] * (n // LANES), axis=1)
    sin_lo = jnp.concatenate([sin_lo] * (n // LANES), axis=1)
    sin_hi = jnp.concatenate([sin_hi] * (n // LANES), axis=1)
    half = ROPE_DIM // 2
    up = pltpu.roll(z, n - half, 1)
    dn = pltpu.roll(z, half, 1)
    return z * cos + up * sin_lo + dn * sin_hi


def _in_proj_kernel(sample, x_ref, g_ref, w_ref, wg_ref, bg_ref, cos_ref, slo_ref, shi_ref, *outs):
    tm = x_ref.shape[0]
    hb = _bf(_rms(x_ref[...], g_ref[...]))

    def proj(off, width):
        return jnp.dot(hb, w_ref[:, off:off + width], preferred_element_type=jnp.float32)

    cos, slo, shi = cos_ref[...], slo_ref[...], shi_ref[...]
    q = _rope(proj(_OFF_QA, D_A), cos, slo, shi)
    k = _rope(proj(_OFF_KA, D_A), cos, slo, shi)
    v = proj(_OFF_VA, D_A)
    if sample:
        (q_hm, k_hm, v_hm, qb_o, kb_o, vb_o, gb_o, gk_o) = outs
        nseq = q_hm.shape[0]
        for h in range(N_HEADS_A):
            sl = slice(h * HEAD_DIM_A, (h + 1) * HEAD_DIM_A)
            q_hm[:, h] = q[:, sl].reshape(nseq, tm // nseq, HEAD_DIM_A)
            k_hm[:, h] = k[:, sl].reshape(nseq, tm // nseq, HEAD_DIM_A)
            v_hm[:, h] = v[:, sl].reshape(nseq, tm // nseq, HEAD_DIM_A)
    else:
        (qs_o, qf_o, kbf_o, vtb_o, kt_o, vt_o, mean_o, qb_o, kb_o, vb_o, gb_o, gk_o) = outs
        qs_o[...] = _bf(q * (HEAD_DIM_A ** -0.5 * LOG2_E))
        qf_o[...] = q
        kbf_o[...] = _bf(k)
        vt = v.T
        kt_o[0] = k.T
        vt_o[0] = vt
        vtb_o[0, 0] = _bf(vt)
        mean_o[0, 0] = jnp.sum(k, axis=0, keepdims=True) * (1.0 / tm)
    qb_o[...] = proj(_OFF_QB, D_BK) * (DK_B ** -0.5)
    kb_o[...] = proj(_OFF_KB, D_BK)
    vb_o[...] = proj(_OFF_VB, D_BV)
    gb_o[...] = proj(_OFF_GB, D_BV)
    lr = _bf(proj(_OFF_LR, LANES))
    zg = jnp.dot(lr, wg_ref[...], preferred_element_type=jnp.float32) + bg_ref[...]
    log_sig = jnp.minimum(zg, 0.0) - jnp.log1p(jnp.exp(-jnp.abs(zg)))
    gk_o[...] = log_sig * (1.0 / GATE_NORMALIZER)


def _in_proj(x2d, norm_mix, w_in_p, w_gk2_p, b_gk, tables, *, sample, nseq, seq_len):
    n = x2d.shape[0]
    tm = TOKEN_TILE
    assert n % tm == 0
    nt = n // tm
    cos, slo, shi = tables
    f32 = jnp.float32
    row = lambda w: pl.BlockSpec((tm, w), lambda i: (i, 0))
    full = lambda a: pl.BlockSpec(a.shape, lambda i: (0,) * a.ndim)
    gla_shapes = [jax.ShapeDtypeStruct((n, D_BK), f32), jax.ShapeDtypeStruct((n, D_BK), f32),
                  jax.ShapeDtypeStruct((n, D_BV), f32), jax.ShapeDtypeStruct((n, D_BV), f32),
                  jax.ShapeDtypeStruct((n, D_BK), f32)]
    gla_specs = [row(D_BK), row(D_BK), row(D_BV), row(D_BV), row(D_BK)]
    if sample:
        assert tm % seq_len == 0
        spt = tm // seq_len
        hm = jax.ShapeDtypeStruct((nseq, N_HEADS_A, seq_len, HEAD_DIM_A), f32)
        hm_spec = pl.BlockSpec((spt, N_HEADS_A, seq_len, HEAD_DIM_A), lambda i: (i, 0, 0, 0))
        out_shape = [hm, hm, hm] + gla_shapes
        out_specs = [hm_spec, hm_spec, hm_spec] + gla_specs
        tab_spec = row(LANES)
    else:
        assert seq_len % tm == 0 and tm == MOBA_BLOCK
        tps = seq_len // tm
        nb = tps
        chan_major = jax.ShapeDtypeStruct((nseq, D_A, seq_len), f32)
        chan_spec = pl.BlockSpec((1, D_A, tm), lambda i: (i // tps, 0, i % tps))
        out_shape = [jax.ShapeDtypeStruct((n, D_A), jnp.bfloat16),
                     jax.ShapeDtypeStruct((n, D_A), f32),
                     jax.ShapeDtypeStruct((n, D_A), jnp.bfloat16),
                     jax.ShapeDtypeStruct((nseq, nb, D_A, tm), jnp.bfloat16),
                     chan_major, chan_major,
                     jax.ShapeDtypeStruct((nseq, nb, 1, D_A), f32)] + gla_shapes
        out_specs = [row(D_A), row(D_A), row(D_A),
                     pl.BlockSpec((1, 1, D_A, tm), lambda i: (i // tps, i % tps, 0, 0)),
                     chan_spec, chan_spec,
                     pl.BlockSpec((1, 1, 1, D_A), lambda i: (i // tps, i % tps, 0, 0))] + gla_specs
        tab_spec = pl.BlockSpec((tm, LANES), lambda i: (i % tps, 0))
    return pl.pallas_call(
        functools.partial(_in_proj_kernel, sample),
        grid=(nt,),
        in_specs=[row(D_MODEL), full(norm_mix), full(w_in_p), full(w_gk2_p), full(b_gk),
                  tab_spec, tab_spec, tab_spec],
        out_specs=out_specs,
        out_shape=out_shape,
        compiler_params=pltpu.CompilerParams(dimension_semantics=("arbitrary",),
                                             vmem_limit_bytes=VMEM_LIMIT),
        name="in_proj_sample" if sample else "in_proj_prompt",
    )(x2d, norm_mix, w_in_p, w_gk2_p, b_gk, cos, slo, shi)


def _rope_tables(pos):
    half = ROPE_DIM // 2
    d = lax.broadcasted_iota(jnp.int32, (1, LANES), 1) % HEAD_DIM_A
    inv = ROPE_THETA ** (-(d % half).astype(jnp.float32) / half)
    ang = pos.astype(jnp.float32)[:, None] * inv
    c, s = jnp.cos(ang), jnp.sin(ang)
    cos = jnp.where(d < ROPE_DIM, c, 1.0)
    sin_lo = jnp.where(d < half, -s, 0.0)
    sin_hi = jnp.where((d >= half) & (d < ROPE_DIM), s, 0.0)
    return cos, sin_lo, sin_hi


def _rank_rows(s, blk, nb):
    rank = jnp.zeros(s.shape, jnp.float32)
    for jp in range(nb):
        row = s[jp:jp + 1, :]
        before = (row > s) | ((row == s) & (blk > jp))
        rank = rank + jnp.where(before, 1.0, 0.0)
    return rank


def _moba_kernel(qs_ref, qf_ref, k_ref, vt_ref, mean_ref, oh_ref, o_ref, qm_ref, sel_ref, m_ref,
                 l_ref, acc_ref, s_ref, p_ref):
    i = pl.program_id(1)
    bq = qs_ref.shape[1]
    nb = mean_ref.shape[1]
    hd = HEAD_DIM_A
    nh = N_HEADS_A
    lane = lax.broadcasted_iota(jnp.int32, (bq, LANES), 1)
    halves = (lane < hd, lane >= hd)
    blk = lax.broadcasted_iota(jnp.int32, (nb, bq), 0)
    pair = lambda h: slice((h // 2) * LANES, (h // 2 + 1) * LANES)

    row0 = pl.multiple_of(i * bq, bq)
    keep = [_bf(jnp.where(hm, 1.0, 0.0)) for hm in halves]
    for h in range(nh):
        qm_ref[h, :, 0:LANES] = qs_ref[0, :, pair(h)] * keep[h % 2]
    for h in range(nh):
        kd = k_ref[0, pl.ds(row0, bq), pair(h)]
        s_ref[h] = lax.dot_general(kd, qm_ref[h, :, 0:LANES], _NT,
                                   preferred_element_type=jnp.float32)
    mlane = lax.broadcasted_iota(jnp.int32, (nb, LANES), 1)
    for h in range(nh):
        mh = jnp.where((mlane >= hd) == bool(h % 2), mean_ref[0, :, pair(h)], 0.0)
        sel_ref[h] = lax.dot_general(mh, qf_ref[0, :, pair(h)], _NT, precision=_HI,
                                     preferred_element_type=jnp.float32)
    for h in range(nh):
        s = jnp.where(blk < i, sel_ref[h], NEG)
        rank = _rank_rows(s, blk, nb)
        bias = jnp.where((rank < MOBA_TOPK) & (blk < i), 0.0, NEG)
        bias = jnp.concatenate([bias, jnp.zeros((LANES - nb, bq), jnp.float32)], axis=0)
        qm_ref[h, :, LANES:2 * LANES] = _bf(bias.T)

    kidx = lax.broadcasted_iota(jnp.int32, (bq, bq), 0)
    qidx = lax.broadcasted_iota(jnp.int32, (bq, bq), 1)
    for h in range(nh):
        s = jnp.where(kidx <= qidx, s_ref[h], NEG)
        m = jnp.max(s, axis=0, keepdims=True)
        p = jnp.exp2(s - m)
        m_ref[h] = m
        l_ref[h] = jnp.sum(p, axis=0, keepdims=True)
        p_ref[h] = _bf(p)
    for h in range(nh):
        acc_ref[h] = jnp.dot(vt_ref[0, i, h * hd:(h + 1) * hd, :], p_ref[h],
                             preferred_element_type=jnp.float32)

    def past(j, carry):
        rowj = pl.multiple_of(j * bq, bq)
        onehot = oh_ref[j]
        for h in range(nh):
            kj = jnp.concatenate([k_ref[0, pl.ds(rowj, bq), pair(h)], onehot], axis=1)
            s_ref[h] = lax.dot_general(kj, qm_ref[h], _NT, preferred_element_type=jnp.float32)
        alphas = []
        for h in range(nh):
            m_old = m_ref[h]
            m_new = jnp.maximum(m_old, jnp.max(s_ref[h], axis=0, keepdims=True))
            alpha = jnp.exp2(m_old - m_new)
            p = jnp.exp2(s_ref[h] - m_new)
            m_ref[h] = m_new
            l_ref[h] = alpha * l_ref[h] + jnp.sum(p, axis=0, keepdims=True)
            p_ref[h] = _bf(p)
            alphas.append(alpha)
        for h in range(nh):
            acc_ref[h] = alphas[h] * acc_ref[h] + jnp.dot(
                vt_ref[0, j, h * hd:(h + 1) * hd, :], p_ref[h], preferred_element_type=jnp.float32)
        return carry

    lax.fori_loop(0, i, past, 0)
    for p2 in range(nh // 2):
        ot = jnp.concatenate([acc_ref[2 * p2] / l_ref[2 * p2],
                              acc_ref[2 * p2 + 1] / l_ref[2 * p2 + 1]], axis=0)
        o_ref[0, :, p2 * LANES:(p2 + 1) * LANES] = ot.T


def _moba_prompt(qs, qf, kbf, vtb, means):
    b, t, _ = qs.shape
    nb = t // MOBA_BLOCK
    bq = MOBA_BLOCK
    assert nb <= LANES
    onehot = jnp.asarray(np.arange(LANES)[None, None, :] == np.arange(nb)[:, None, None],
                         jnp.bfloat16) * jnp.ones((nb, bq, LANES), jnp.bfloat16)
    return pl.pallas_call(
        _moba_kernel,
        grid=(b, nb),
        in_specs=[pl.BlockSpec((1, bq, D_A), lambda b_, i: (b_, i, 0)),
                  pl.BlockSpec((1, bq, D_A), lambda b_, i: (b_, i, 0)),
                  pl.BlockSpec((1, t, D_A), lambda b_, i: (b_, 0, 0)),
                  pl.BlockSpec((1, nb, D_A, bq), lambda b_, i: (b_, 0, 0, 0)),
                  pl.BlockSpec((1, nb, D_A), lambda b_, i: (b_, 0, 0)),
                  pl.BlockSpec((nb, bq, LANES), lambda b_, i: (0, 0, 0))],
        out_specs=pl.BlockSpec((1, bq, D_A), lambda b_, i: (b_, i, 0)),
        out_shape=jax.ShapeDtypeStruct((b, t, D_A), jnp.float32),
        scratch_shapes=[pltpu.VMEM((N_HEADS_A, bq, 2 * LANES), jnp.bfloat16),
                        pltpu.VMEM((N_HEADS_A, nb, bq), jnp.float32),
                        pltpu.VMEM((N_HEADS_A, 1, bq), jnp.float32),
                        pltpu.VMEM((N_HEADS_A, 1, bq), jnp.float32),
                        pltpu.VMEM((N_HEADS_A, HEAD_DIM_A, bq), jnp.float32),
                        pltpu.VMEM((N_HEADS_A, bq, bq), jnp.float32),
                        pltpu.VMEM((N_HEADS_A, bq, bq), jnp.bfloat16)],
        compiler_params=pltpu.CompilerParams(dimension_semantics=("arbitrary", "arbitrary"),
                                             vmem_limit_bytes=VMEM_LIMIT),
        name="moba_prompt",
    )(qs, qf, kbf, vtb, means, onehot)


def _pages_copy(cache_ref, layer, page, head0, buf, slot, pg, sem):
    hg = buf.shape[1]
    return pltpu.make_async_copy(cache_ref.at[layer, page, pl.ds(head0, hg)],
                                 buf.at[slot, :, :, pl.ds(pg * LANES, LANES)], sem)


def _moba_sample_kernel(layer, nseq, n_pages, pt_ref, q_ref, kn_ref, vn_ref, ck_ref, cv_ref, o_ref,
                        kbuf, vbuf, sem, s_ref, p_ref):
    b, g = pl.program_id(0), pl.program_id(1)
    ngrp = pl.num_programs(1)
    hg, hd = kbuf.shape[1], kbuf.shape[2]
    nkey = kbuf.shape[3]
    nblk = nkey // MOBA_BLOCK
    ppb = MOBA_BLOCK // LANES
    step = b * ngrp + g
    last = nseq * ngrp - 1
    nslot = kbuf.shape[0]
    ahead = nslot - 1
    slot = step % nslot

    def issue(bb, gg, sl, pg):
        page = pt_ref[bb * n_pages + pg]
        _pages_copy(ck_ref, layer, page, gg * hg, kbuf, sl, pg, sem.at[0, sl]).start()
        _pages_copy(cv_ref, layer, page, gg * hg, vbuf, sl, pg, sem.at[1, sl]).start()

    def wait_all(sl):
        for pg in range(n_pages):
            _pages_copy(ck_ref, layer, 0, 0, kbuf, sl, pg, sem.at[0, sl]).wait()
        for pg in range(n_pages):
            _pages_copy(cv_ref, layer, 0, 0, vbuf, sl, pg, sem.at[1, sl]).wait()

    @pl.when(step == 0)
    def _():
        for d in range(ahead):
            first = jnp.minimum(d, last)
            for pg in range(n_pages):
                issue(first // ngrp, first % ngrp, d, pg)

    wait_all(slot)
    nxt = jnp.minimum(step + ahead, last)
    nxt_b, nxt_g = nxt // ngrp, nxt % ngrp
    nxt_slot = (step + ahead) % nslot

    ts = q_ref.shape[2]
    lane = lax.broadcasted_iota(jnp.int32, (ts, LANES), 1)
    trow = lax.broadcasted_iota(jnp.int32, (ts, LANES), 0)
    mlane = lax.broadcasted_iota(jnp.int32, (hd, LANES), 1)
    pad = jnp.zeros((LANES - ts, hd), jnp.float32)
    heads = range(hg)
    mts = []
    for h in heads:
        mt = jnp.zeros((hd, LANES), jnp.float32)
        for j in range(nblk):
            x = kbuf[slot, h, :, j * MOBA_BLOCK:j * MOBA_BLOCK + LANES]
            for pp in range(1, ppb):
                x = x + kbuf[slot, h, :, j * MOBA_BLOCK + pp * LANES:j * MOBA_BLOCK + (pp + 1) * LANES]
            mt = jnp.where(mlane == j, jnp.sum(x, axis=1, keepdims=True), mt)
            if h == 0:
                for pp in range(ppb):
                    issue(nxt_b, nxt_g, nxt_slot, j * ppb + pp)
        mts.append(mt * (1.0 / MOBA_BLOCK))
    qs = [q_ref[0, h] for h in heads]
    sbs = [jnp.where(lane < nblk,
                     jnp.dot(qs[h], mts[h], precision=_HI, preferred_element_type=jnp.float32), NEG)
           for h in heads]
    qbs = [_bf(q * (hd ** -0.5)) for q in qs]
    for h in heads:
        s_ref[h] = jnp.dot(qbs[h], _bf(kbuf[slot, h]), preferred_element_type=jnp.float32)
    sels = []
    for h in heads:
        rank = jnp.zeros((ts, LANES), jnp.float32)
        for jp in range(nblk):
            col = sbs[h][:, jp:jp + 1]
            before = (col > sbs[h]) | ((col == sbs[h]) & (lane > jp))
            rank = rank + jnp.where(before, 1.0, 0.0)
        sels.append(jnp.where((rank < MOBA_TOPK) & (lane < nblk), 1.0, 0.0))
    vns, pos, ls = [], [], []
    for h in heads:
        s = jnp.concatenate(
            [jnp.where(sels[h][:, j:j + 1] > 0.0, s_ref[h, :, j * MOBA_BLOCK:(j + 1) * MOBA_BLOCK],
                       NEG) for j in range(nblk)], axis=1)
        kn = _bf(jnp.concatenate([kn_ref[0, h], pad], axis=0))
        vns.append(_bf(jnp.concatenate([vn_ref[0, h], pad], axis=0)))
        so = lax.dot_general(qbs[h], kn, _NT, preferred_element_type=jnp.float32)
        so = jnp.where(lane <= trow, so, NEG)
        m = jnp.maximum(jnp.max(s, axis=1, keepdims=True), jnp.max(so, axis=1, keepdims=True))
        p = jnp.exp(s - m)
        po = jnp.exp(so - m)
        ls.append(jnp.sum(p, axis=1, keepdims=True) + jnp.sum(po, axis=1, keepdims=True))
        p_ref[h] = _bf(p)
        pos.append(_bf(po))
    for h in heads:
        o = (lax.dot_general(p_ref[h], _bf(vbuf[slot, h]), _NT, preferred_element_type=jnp.float32)
             + jnp.dot(pos[h], vns[h], preferred_element_type=jnp.float32))
        o_ref[0, h] = o / ls[h]

    @pl.when(step == last)
    def _():
        for d in range(1, nslot):
            wait_all((step + d) % nslot)


def _moba_sample(layer, page_table, q_hm, k_hm, v_hm, cache_kt, cache_vt):
    db, nh, ts, hd = q_hm.shape
    n_pages = page_table.shape[1]
    page = cache_kt.shape[4]
    hg = SAMPLE_HEAD_GROUP
    assert page == LANES and MOBA_BLOCK % page == 0 and nh % hg == 0
    assert (n_pages * page) % MOBA_BLOCK == 0 and n_pages * page // MOBA_BLOCK <= LANES
    blk = pl.BlockSpec((1, hg, ts, hd), lambda b, g, pt: (b, g, 0, 0))
    grid_spec = pltpu.PrefetchScalarGridSpec(
        num_scalar_prefetch=1,
        grid=(db, nh // hg),
        in_specs=[blk, blk, blk, pl.BlockSpec(memory_space=pl.ANY), pl.BlockSpec(memory_space=pl.ANY)],
        out_specs=blk,
        scratch_shapes=[pltpu.VMEM((SAMPLE_SLOTS, hg, hd, n_pages * page), jnp.float32),
                        pltpu.VMEM((SAMPLE_SLOTS, hg, hd, n_pages * page), jnp.float32),
                        pltpu.SemaphoreType.DMA((2, SAMPLE_SLOTS)),
                        pltpu.VMEM((hg, ts, n_pages * page), jnp.float32),
                        pltpu.VMEM((hg, ts, n_pages * page), jnp.bfloat16)])
    return pl.pallas_call(
        functools.partial(_moba_sample_kernel, layer, db, n_pages),
        grid_spec=grid_spec,
        out_shape=jax.ShapeDtypeStruct((db, nh, ts, hd), jnp.float32),
        compiler_params=pltpu.CompilerParams(dimension_semantics=("arbitrary", "arbitrary"),
                                             vmem_limit_bytes=VMEM_LIMIT),
        name="moba_sample",
    )(page_table.reshape(-1), q_hm, k_hm, v_hm, cache_kt, cache_vt)


def _gla_chunks(c, qs, ks, gs, vs, sts, seg, ltri, hmask, trow):
    dk, dv = DK_B, DV_B
    n = len(qs)
    heads = range(N_HEADS_B)
    f32 = jnp.float32
    zero = jnp.zeros((c, D_BK), f32)
    cums = [jnp.dot(ltri, g, precision=_HI, preferred_element_type=f32) for g in gs]
    pmats = []
    for q, k, cum in zip(qs, ks, cums):
        parts = []
        for s in range(c):
            e = jnp.exp(jnp.where(trow >= s, cum - cum[s:s + 1, :], NEG))
            parts.append(q * k[s:s + 1, :] * e)
        pmats.append(_bf(jnp.concatenate(parts, axis=0)))
    rs = [jnp.dot(pm, seg, preferred_element_type=f32) for pm in pmats]
    by_head = lambda x: _bf(jnp.concatenate([jnp.where(hmask[h], x, zero) for h in heads], axis=0))
    qes = [by_head(q * jnp.exp(cum)) for q, cum in zip(qs, cums)]
    kds = [by_head(k * jnp.exp(cum[c - 1:c, :] - cum)) for k, cum in zip(ks, cums)]
    inters = [lax.dot_general(qes[i], _bf(sts[i]), _NT, preferred_element_type=f32)
              for i in range(n)]
    vstk = [_bf(jnp.concatenate([v[:, h * dv:(h + 1) * dv] for h in heads], axis=0)) for v in vs]
    upds = [lax.dot_general(vstk[i], kds[i], _TN, preferred_element_type=f32) for i in range(n)]
    outs, new = [], []
    for i in range(n):
        r, v = rs[i], vs[i]
        o = r[0:c] * v[0:1, :]
        for s in range(1, c):
            o = o + r[s * c:(s + 1) * c] * v[s:s + 1, :]
        outs.append(o + jnp.concatenate([inters[i][h * c:(h + 1) * c] for h in heads], axis=1))
        new.append(sts[i] * jnp.exp(cums[i][c - 1:c, :]) + upds[i])
    return outs, new


def _gla_kernel(chunk, q_ref, k_ref, g_ref, v_ref, s0_ref, seg_ref, o_ref, sout_ref, s_ref):
    tb = pl.program_id(1)
    c = chunk
    nsb = q_ref.shape[0]
    nchunk = q_ref.shape[1] // c

    @pl.when(tb == 0)
    def _():
        for sq in range(nsb):
            s_ref[sq] = s0_ref[sq].T

    rows = lax.broadcasted_iota(jnp.int32, (c, c), 0)
    cols = lax.broadcasted_iota(jnp.int32, (c, c), 1)
    ltri = jnp.where(rows >= cols, 1.0, 0.0)
    trow = lax.broadcasted_iota(jnp.int32, (c, D_BK), 0)
    lane = lax.broadcasted_iota(jnp.int32, (c, D_BK), 1)
    hmask = [(lane >= h * DK_B) & (lane < (h + 1) * DK_B) for h in range(N_HEADS_B)]

    def body(ci, carry):
        t0 = pl.multiple_of(ci * c, c)
        tok = lambda ref: [ref[sq, pl.ds(t0, c), :] for sq in range(nsb)]
        outs, new = _gla_chunks(c, tok(q_ref), tok(k_ref), tok(g_ref), tok(v_ref),
                                [s_ref[sq] for sq in range(nsb)], seg_ref[...], ltri, hmask, trow)
        for sq in range(nsb):
            o_ref[sq, pl.ds(t0, c), :] = outs[sq]
            s_ref[sq] = new[sq]
        return carry

    lax.fori_loop(0, nchunk, body, 0)

    @pl.when(tb == pl.num_programs(1) - 1)
    def _():
        for sq in range(nsb):
            sout_ref[sq] = s_ref[sq].T


def _gla(qb, kb, gk, vb, s0, seg, *, chunk, tblock, nsb):
    nseq, t, _ = qb.shape
    assert t % tblock == 0 and tblock % chunk == 0 and nseq % nsb == 0
    ntb = t // tblock
    tok = lambda w: pl.BlockSpec((nsb, tblock, w), lambda s, tb: (s, tb, 0))
    st_spec = pl.BlockSpec((nsb, D_BK, DV_B), lambda s, tb: (s, 0, 0))
    return pl.pallas_call(
        functools.partial(_gla_kernel, chunk),
        grid=(nseq // nsb, ntb),
        in_specs=[tok(D_BK), tok(D_BK), tok(D_BK), tok(D_BV), st_spec,
                  pl.BlockSpec(seg.shape, lambda s, tb: (0, 0))],
        out_specs=[tok(D_BV), st_spec],
        out_shape=[jax.ShapeDtypeStruct((nseq, t, D_BV), jnp.float32),
                   jax.ShapeDtypeStruct((nseq, D_BK, DV_B), jnp.float32)],
        scratch_shapes=[pltpu.VMEM((nsb, DV_B, D_BK), jnp.float32)],
        compiler_params=pltpu.CompilerParams(dimension_semantics=("arbitrary", "arbitrary"),
                                             vmem_limit_bytes=VMEM_LIMIT),
        name="gla_c%d" % chunk,
    )(qb, kb, gk, vb, s0, seg)


def _post_kernel(x_ref, oa_ref, ob_ref, gb_ref, gn_ref, wo_ref, nm_ref, wu_ref, wd_ref, nf_ref,
                 y_ref, x1_ref, h2_ref, acc_ref):
    kf = pl.program_id(1)

    @pl.when(kf == 0)
    def _():
        ob = ob_ref[...]
        gb = gb_ref[...]
        gn = gn_ref[...]
        heads = []
        for h in range(N_HEADS_B):
            sl = slice(h * DV_B, (h + 1) * DV_B)
            gate = gb[:, sl]
            heads.append(_rms(ob[:, sl], gn[:, sl]) * (gate * (1.0 / (1.0 + jnp.exp(-gate)))))
        mix = _bf(jnp.concatenate([oa_ref[...]] + heads, axis=1))
        x1 = x_ref[...] + jnp.dot(mix, wo_ref[...], preferred_element_type=jnp.float32)
        x1_ref[...] = x1
        h2_ref[...] = _bf(_rms(x1, nm_ref[...]))
        acc_ref[...] = jnp.zeros_like(acc_ref)

    u = jnp.maximum(jnp.dot(h2_ref[...], wu_ref[...], preferred_element_type=jnp.float32), 0.0)
    acc_ref[...] += jnp.dot(_bf(u * u), wd_ref[...], preferred_element_type=jnp.float32)

    @pl.when(kf == pl.num_programs(1) - 1)
    def _():
        y_ref[...] = _rms(x1_ref[...] + acc_ref[...], nf_ref[...])


def _post(x2d, oa, ob, gb, gla_norm, w_o, norm_mlp, w_up, w_down, norm_final):
    n = x2d.shape[0]
    tm = POST_TILE if n % POST_TILE == 0 else TOKEN_TILE
    tf = FF_TILE
    assert n % tm == 0 and D_FF % tf == 0
    row = lambda w: pl.BlockSpec((tm, w), lambda i, kf: (i, 0))
    full = lambda a: pl.BlockSpec(a.shape, lambda i, kf: (0,) * a.ndim)
    return pl.pallas_call(
        _post_kernel,
        grid=(n // tm, D_FF // tf),
        in_specs=[row(D_MODEL), row(D_A), row(D_BV), row(D_BV), full(gla_norm), full(w_o),
                  full(norm_mlp),
                  pl.BlockSpec((D_MODEL, tf), lambda i, kf: (0, kf)),
                  pl.BlockSpec((tf, D_MODEL), lambda i, kf: (kf, 0)),
                  full(norm_final)],
        out_specs=row(D_MODEL),
        out_shape=jax.ShapeDtypeStruct((n, D_MODEL), jnp.float32),
        scratch_shapes=[pltpu.VMEM((tm, D_MODEL), jnp.float32),
                        pltpu.VMEM((tm, D_MODEL), jnp.bfloat16),
                        pltpu.VMEM((tm, D_MODEL), jnp.float32)],
        compiler_params=pltpu.CompilerParams(dimension_semantics=("arbitrary", "arbitrary"),
                                             vmem_limit_bytes=VMEM_LIMIT),
        name="post",
    )(x2d, oa, ob, gb, gla_norm, w_o, norm_mlp, w_up, w_down, norm_final)


def _seg_matrix():
    d = np.arange(D_BK)[:, None] // DK_B
    e = np.arange(D_BV)[None, :] // DV_B
    return jnp.asarray(d == e, jnp.bfloat16)


def _layer(l, x_prompt, x_sample, cache_k, cache_v, state_gla, page_table, norm_mix, w_in, w_gk2,
           b_gk, gla_norm, w_o, norm_mlp, w_up, w_down):
    b, t_p, _ = x_prompt.shape
    db, t_s, _ = x_sample.shape
    page = cache_k.shape[3]
    n_pages = page_table.shape[1]
    past = n_pages * page
    assert past % MOBA_BLOCK == 0 and past // MOBA_BLOCK >= MOBA_TOPK
    assert t_s <= 8 and TOKEN_TILE % t_s == 0

    w_in_p = _bf(jnp.concatenate(
        [w_in[l], jnp.zeros((D_MODEL, D_IN_PAD - D_IN), w_in.dtype)], axis=1))
    w_gk2_p = _bf(jnp.concatenate(
        [w_gk2[l], jnp.zeros((LANES - GATE_RANK, D_BK), w_gk2.dtype)], axis=0))
    nmix = norm_mix[l].reshape(1, D_MODEL)
    bg = b_gk[l].reshape(1, D_BK)
    gn = gla_norm[l].reshape(1, D_BV)
    nmlp = norm_mlp[l].reshape(1, D_MODEL)
    wo, wu, wd = _bf(w_o[l]), _bf(w_up[l]), _bf(w_down[l])
    seg = _seg_matrix()

    xp2 = x_prompt.reshape(b * t_p, D_MODEL)
    tab_p = _rope_tables(jnp.arange(t_p, dtype=jnp.int32))
    (qs, qf, kbf, vtb, kt_p, vt_p, means, qb, kb, vb, gb, gk) = _in_proj(
        xp2, nmix, w_in_p, w_gk2_p, bg, tab_p, sample=False, nseq=b, seq_len=t_p)
    nb = t_p // MOBA_BLOCK
    oa_p = _moba_prompt(qs.reshape(b, t_p, D_A), qf.reshape(b, t_p, D_A), kbf.reshape(b, t_p, D_A),
                        vtb, means.reshape(b, nb, D_A))
    s0_p = jnp.zeros((b, D_BK, DV_B), state_gla.dtype)
    ob_p, s_p = _gla(qb.reshape(b, t_p, D_BK), kb.reshape(b, t_p, D_BK), gk.reshape(b, t_p, D_BK),
                     vb.reshape(b, t_p, D_BV), s0_p, seg, chunk=GLA_CHUNK_PROMPT,
                     tblock=min(GLA_TBLOCK, t_p), nsb=b)
    prompt = (xp2, oa_p.reshape(b * t_p, D_A), ob_p.reshape(b * t_p, D_BV), gb)
    to_rows = lambda a: jnp.swapaxes(a.reshape(b, N_HEADS_A, HEAD_DIM_A, t_p), 2, 3)
    k_p, v_p = to_rows(kt_p), to_rows(vt_p)

    xs2 = x_sample.reshape(db * t_s, D_MODEL)
    pos_s = past + jnp.tile(jnp.arange(t_s, dtype=jnp.int32), db)
    tab_s = _rope_tables(pos_s)
    (q_s, k_s, v_s, qb_s, kb_s, vb_s, gb_s, gk_s) = _in_proj(
        xs2, nmix, w_in_p, w_gk2_p, bg, tab_s, sample=True, nseq=db, seq_len=t_s)
    oa_s = _moba_sample(l, page_table, q_s, k_s, v_s,
                        jnp.swapaxes(cache_k, 3, 4), jnp.swapaxes(cache_v, 3, 4))
    oa_s = oa_s.transpose(0, 2, 1, 3).reshape(db * t_s, D_A)
    nsb = GLA_SEQS_PER_STEP if db % GLA_SEQS_PER_STEP == 0 else 1
    ob_s, s_s = _gla(qb_s.reshape(db, t_s, D_BK), kb_s.reshape(db, t_s, D_BK),
                     gk_s.reshape(db, t_s, D_BK), vb_s.reshape(db, t_s, D_BV),
                     state_gla[l].reshape(db, D_BK, DV_B), seg, chunk=t_s, tblock=t_s, nsb=nsb)
    sample = (xs2, oa_s, ob_s.reshape(db * t_s, D_BV), gb_s)
    return prompt, sample, (gn, wo, nmlp, wu, wd), (k_p, v_p, s_p, k_s, v_s, s_s)


def kernel(x_prompt, x_sample, cache_k, cache_v, state_gla, page_table, norm_mix, w_in, w_gk2, b_gk,
           gla_norm, w_o, norm_mlp, w_up, w_down, norm_final):
    depth = norm_mix.shape[0]
    assert depth == 1, "the trunk is one layer deep"
    b, t_p, _ = x_prompt.shape
    db, t_s, _ = x_sample.shape
    prompt, sample, weights, new = _layer(0, x_prompt, x_sample, cache_k, cache_v, state_gla,
                                          page_table, norm_mix, w_in, w_gk2, b_gk, gla_norm, w_o,
                                          norm_mlp, w_up, w_down)
    nf = norm_final.reshape(1, D_MODEL)
    y_p = _post(*prompt, *weights, nf).reshape(b, t_p, D_MODEL)
    y_s = _post(*sample, *weights, nf).reshape(db, t_s, D_MODEL)
    k_p, v_p, s_p, k_s, v_s, s_s = new
    st = lambda s: s.reshape(1, s.shape[0], N_HEADS_B, DK_B, DV_B)
    return (y_p, y_s, k_p[None], v_p[None], st(s_p), k_s[None], v_s[None], st(s_s))
```

```python
import functools

import jax
import jax.numpy as jnp
import numpy as np
from jax import lax
from jax.experimental import pallas as pl
from jax.experimental.pallas import tpu as pltpu

D_MODEL = 1024
HEAD_DIM_A = 64
D_A = D_MODEL // 2
N_HEADS_A = D_A // HEAD_DIM_A
MOBA_BLOCK = 256
MOBA_TOPK = 3
ROPE_DIM = HEAD_DIM_A // 4
ROPE_THETA = 500000.0
D_BV = D_MODEL // 2
N_HEADS_B = 4
DV_B = D_BV // N_HEADS_B
DK_B = DV_B // 2
D_BK = N_HEADS_B * DK_B
GATE_RANK = 16
GATE_NORMALIZER = 16.0
D_IN = 3 * D_A + 2 * D_BK + 2 * D_BV + GATE_RANK
D_FF = 4 * D_MODEL
EPS = 1e-6
NEG = -1e30
LOG2_E = 1.4426950408889634

LANES = 128
D_IN_PAD = D_IN - GATE_RANK + LANES
_OFF_QA, _OFF_KA, _OFF_VA = 0, D_A, 2 * D_A
_OFF_QB = 3 * D_A
_OFF_KB = _OFF_QB + D_BK
_OFF_VB = _OFF_KB + D_BK
_OFF_GB = _OFF_VB + D_BV
_OFF_LR = _OFF_GB + D_BV

TOKEN_TILE = 256
POST_TILE = 512
GLA_CHUNK_PROMPT = 16
GLA_TBLOCK = 256
GLA_SEQS_PER_STEP = 8
FF_TILE = 1024
SAMPLE_HEAD_GROUP = 2
SAMPLE_SLOTS = 3
VMEM_LIMIT = 48 * 1024 * 1024

_HI = lax.Precision.HIGHEST
_NT = (((1,), (1,)), ((), ()))
_TN = (((0,), (0,)), ((), ()))


def _bf(x):
    return x.astype(jnp.bfloat16)


def _rms(x, g):
    r = lax.rsqrt(jnp.mean(x * x, axis=-1, keepdims=True) + EPS)
    return (x * r) * g


def _rope(z, cos, sin_lo, sin_hi):
    n = z.shape[1]
    cos = jnp.concatenate([cos] * (n // LANES), axis=1)
    sin_lo = jnp.concatenate([sin_lo] * (n // LANES), axis=1)
    sin_hi = jnp.concatenate([sin_hi] * (n // LANES), axis=1)
    half = ROPE_DIM // 2
    up = pltpu.roll(z, n - half, 1)
    dn = pltpu.roll(z, half, 1)
    return z * cos + up * sin_lo + dn * sin_hi


def _in_proj_kernel(sample, x_ref, g_ref, w_ref, wg_ref, bg_ref, cos_ref, slo_ref, shi_ref, *outs):
    tm = x_ref.shape[0]
    hb = _bf(_rms(x_ref[...], g_ref[...]))

    def proj(off, width):
        return jnp.dot(hb, w_ref[:, off:off + width], preferred_element_type=jnp.float32)

    cos, slo, shi = cos_ref[...], slo_ref[...], shi_ref[...]
    q = _rope(proj(_OFF_QA, D_A), cos, slo, shi)
    k = _rope(proj(_OFF_KA, D_A), cos, slo, shi)
    v = proj(_OFF_VA, D_A)
    if sample:
        (q_hm, k_hm, v_hm, qb_o, kb_o, vb_o, gb_o, gk_o) = outs
        nseq = q_hm.shape[0]
        for h in range(N_HEADS_A):
            sl = slice(h * HEAD_DIM_A, (h + 1) * HEAD_DIM_A)
            q_hm[:, h] = q[:, sl].reshape(nseq, tm // nseq, HEAD_DIM_A)
            k_hm[:, h] = k[:, sl].reshape(nseq, tm // nseq, HEAD_DIM_A)
            v_hm[:, h] = v[:, sl].reshape(nseq, tm // nseq, HEAD_DIM_A)
    else:
        (qs_o, qf_o, kbf_o, vtb_o, kt_o, vt_o, mean_o, qb_o, kb_o, vb_o, gb_o, gk_o) = outs
        qs_o[...] = _bf(q * (HEAD_DIM_A ** -0.5 * LOG2_E))
        qf_o[...] = q
        kbf_o[...] = _bf(k)
        vt = v.T
        kt_o[0] = k.T
        vt_o[0] = vt
        vtb_o[0, 0] = _bf(vt)
        mean_o[0, 0] = jnp.sum(k, axis=0, keepdims=True) * (1.0 / tm)
    qb_o[...] = proj(_OFF_QB, D_BK) * (DK_B ** -0.5)
    kb_o[...] = proj(_OFF_KB, D_BK)
    vb_o[...] = proj(_OFF_VB, D_BV)
    gb_o[...] = proj(_OFF_GB, D_BV)
    lr = _bf(proj(_OFF_LR, LANES))
    zg = jnp.dot(lr, wg_ref[...], preferred_element_type=jnp.float32) + bg_ref[...]
    log_sig = jnp.minimum(zg, 0.0) - jnp.log1p(jnp.exp(-jnp.abs(zg)))
    gk_o[...] = log_sig * (1.0 / GATE_NORMALIZER)


def _in_proj(x2d, norm_mix, w_in_p, w_gk2_p, b_gk, tables, *, sample, nseq, seq_len):
    n = x2d.shape[0]
    tm = TOKEN_TILE
    assert n % tm == 0
    nt = n // tm
    cos, slo, shi = tables
    f32 = jnp.float32
    row = lambda w: pl.BlockSpec((tm, w), lambda i: (i, 0))
    full = lambda a: pl.BlockSpec(a.shape, lambda i: (0,) * a.ndim)
    gla_shapes = [jax.ShapeDtypeStruct((n, D_BK), f32), jax.ShapeDtypeStruct((n, D_BK), f32),
                  jax.ShapeDtypeStruct((n, D_BV), f32), jax.ShapeDtypeStruct((n, D_BV), f32),
                  jax.ShapeDtypeStruct((n, D_BK), f32)]
    gla_specs = [row(D_BK), row(D_BK), row(D_BV), row(D_BV), row(D_BK)]
    if sample:
        assert tm % seq_len == 0
        spt = tm // seq_len
        hm = jax.ShapeDtypeStruct((nseq, N_HEADS_A, seq_len, HEAD_DIM_A), f32)
        hm_spec = pl.BlockSpec((spt, N_HEADS_A, seq_len, HEAD_DIM_A), lambda i: (i, 0, 0, 0))
        out_shape = [hm, hm, hm] + gla_shapes
        out_specs = [hm_spec, hm_spec, hm_spec] + gla_specs
        tab_spec = row(LANES)
    else:
        assert seq_len % tm == 0 and tm == MOBA_BLOCK
        tps = seq_len // tm
        nb = tps
        chan_major = jax.ShapeDtypeStruct((nseq, D_A, seq_len), f32)
        chan_spec = pl.BlockSpec((1, D_A, tm), lambda i: (i // tps, 0, i % tps))
        out_shape = [jax.ShapeDtypeStruct((n, D_A), jnp.bfloat16),
                     jax.ShapeDtypeStruct((n, D_A), f32),
                     jax.ShapeDtypeStruct((n, D_A), jnp.bfloat16),
                     jax.ShapeDtypeStruct((nseq, nb, D_A, tm), jnp.bfloat16),
                     chan_major, chan_major,
                     jax.ShapeDtypeStruct((nseq, nb, 1, D_A), f32)] + gla_shapes
        out_specs = [row(D_A), row(D_A), row(D_A),
                     pl.BlockSpec((1, 1, D_A, tm), lambda i: (i // tps, i % tps, 0, 0)),
                     chan_spec, chan_spec,
                     pl.BlockSpec((1, 1, 1, D_A), lambda i: (i // tps, i % tps, 0, 0))] + gla_specs
        tab_spec = pl.BlockSpec((tm, LANES), lambda i: (i % tps, 0))
    return pl.pallas_call(
        functools.partial(_in_proj_kernel, sample),
        grid=(nt,),
        in_specs=[row(D_MODEL), full(norm_mix), full(w_in_p), full(w_gk2_p), full(b_gk),
                  tab_spec, tab_spec, tab_spec],
        out_specs=out_specs,
        out_shape=out_shape,
        compiler_params=pltpu.CompilerParams(dimension_semantics=("arbitrary",),
                                             vmem_limit_bytes=VMEM_LIMIT),
        name="in_proj_sample" if sample else "in_proj_prompt",
    )(x2d, norm_mix, w_in_p, w_gk2_p, b_gk, cos, slo, shi)


def _rope_tables(pos):
    half = ROPE_DIM // 2
    d = lax.broadcasted_iota(jnp.int32, (1, LANES), 1) % HEAD_DIM_A
    inv = ROPE_THETA ** (-(d % half).astype(jnp.float32) / half)
    ang = pos.astype(jnp.float32)[:, None] * inv
    c, s = jnp.cos(ang), jnp.sin(ang)
    cos = jnp.where(d < ROPE_DIM, c, 1.0)
    sin_lo = jnp.where(d < half, -s, 0.0)
    sin_hi = jnp.where((d >= half) & (d < ROPE_DIM), s, 0.0)
    return cos, sin_lo, sin_hi


def _rank_rows(s, blk, nb):
    rank = jnp.zeros(s.shape, jnp.float32)
    for jp in range(nb):
        row = s[jp:jp + 1, :]
        before = (row > s) | ((row == s) & (blk > jp))
        rank = rank + jnp.where(before, 1.0, 0.0)
    return rank


def _moba_kernel(qs_ref, qf_ref, k_ref, vt_ref, mean_ref, oh_ref, o_ref, qm_ref, sel_ref, m_ref,
                 l_ref, acc_ref, s_ref):
    i = pl.program_id(1)
    bq = qs_ref.shape[1]
    nb = mean_ref.shape[1]
    hd = HEAD_DIM_A
    nh = N_HEADS_A
    lane = lax.broadcasted_iota(jnp.int32, (bq, LANES), 1)
    halves = (lane < hd, lane >= hd)
    blk = lax.broadcasted_iota(jnp.int32, (nb, bq), 0)
    pair = lambda h: slice((h // 2) * LANES, (h // 2 + 1) * LANES)

    row0 = pl.multiple_of(i * bq, bq)
    keep = [_bf(jnp.where(hm, 1.0, 0.0)) for hm in halves]
    for h in range(nh):
        qm_ref[h, :, 0:LANES] = qs_ref[0, :, pair(h)] * keep[h % 2]
    kidx = lax.broadcasted_iota(jnp.int32, (bq, bq), 0)
    qidx = lax.broadcasted_iota(jnp.int32, (bq, bq), 1)
    tops = []
    for h in range(nh):
        kd = k_ref[0, pl.ds(row0, bq), pair(h)]
        s = lax.dot_general(kd, qm_ref[h, :, 0:LANES], _NT, preferred_element_type=jnp.float32)
        s = jnp.where(kidx <= qidx, s, NEG)
        s_ref[h] = s
        tops.append(jnp.max(s, axis=0, keepdims=True))
    mlane = lax.broadcasted_iota(jnp.int32, (nb, LANES), 1)
    for h in range(nh):
        mh = jnp.where((mlane >= hd) == bool(h % 2), mean_ref[0, :, pair(h)], 0.0)
        sel_ref[h] = lax.dot_general(mh, qf_ref[0, :, pair(h)], _NT, precision=_HI,
                                     preferred_element_type=jnp.float32)
    for h in range(nh):
        s = jnp.where(blk < i, sel_ref[h], NEG)
        rank = _rank_rows(s, blk, nb)
        bias = jnp.where((rank < MOBA_TOPK) & (blk < i), 0.0, NEG)
        bias = jnp.concatenate([bias, jnp.zeros((LANES - nb, bq), jnp.float32)], axis=0)
        qm_ref[h, :, LANES:2 * LANES] = _bf(bias.T)

    for h in range(nh):
        p = jnp.exp2(s_ref[h] - tops[h])
        m_ref[h] = tops[h]
        l_ref[h] = jnp.sum(p, axis=0, keepdims=True)
        acc_ref[h] = jnp.dot(vt_ref[0, i, h * hd:(h + 1) * hd, :], _bf(p),
                             preferred_element_type=jnp.float32)

    def past(js):
        tops = []
        for h in range(nh):
            top = None
            for n, j in enumerate(js):
                rowj = pl.multiple_of(j * bq, bq)
                kj = jnp.concatenate([k_ref[0, pl.ds(rowj, bq), pair(h)], oh_ref[j]], axis=1)
                s = lax.dot_general(kj, qm_ref[h], _NT, preferred_element_type=jnp.float32)
                s_ref[n * nh + h] = s
                t = jnp.max(s, axis=0, keepdims=True)
                top = t if top is None else jnp.maximum(top, t)
            tops.append(top)
        for h in range(nh):
            m_old = m_ref[h]
            m_new = jnp.maximum(m_old, tops[h])
            alpha = jnp.exp2(m_old - m_new)
            l_new = alpha * l_ref[h]
            acc = alpha * acc_ref[h]
            for n, j in enumerate(js):
                p = jnp.exp2(s_ref[n * nh + h] - m_new)
                l_new = l_new + jnp.sum(p, axis=0, keepdims=True)
                acc = acc + jnp.dot(vt_ref[0, j, h * hd:(h + 1) * hd, :], _bf(p),
                                    preferred_element_type=jnp.float32)
            m_ref[h] = m_new
            l_ref[h] = l_new
            acc_ref[h] = acc

    def past_pair(t, carry):
        past([2 * t, 2 * t + 1])
        return carry

    lax.fori_loop(0, i // 2, past_pair, 0)

    @pl.when(i % 2 == 1)
    def _():
        past([i - 1])

    for p2 in range(nh // 2):
        ot = jnp.concatenate([acc_ref[2 * p2] / l_ref[2 * p2],
                              acc_ref[2 * p2 + 1] / l_ref[2 * p2 + 1]], axis=0)
        o_ref[0, :, p2 * LANES:(p2 + 1) * LANES] = ot.T


def _moba_prompt(qs, qf, kbf, vtb, means):
    b, t, _ = qs.shape
    nb = t // MOBA_BLOCK
    bq = MOBA_BLOCK
    assert nb <= LANES
    onehot = jnp.asarray(np.arange(LANES)[None, None, :] == np.arange(nb)[:, None, None],
                         jnp.bfloat16) * jnp.ones((nb, bq, LANES), jnp.bfloat16)
    return pl.pallas_call(
        _moba_kernel,
        grid=(b, nb),
        in_specs=[pl.BlockSpec((1, bq, D_A), lambda b_, i: (b_, i, 0)),
                  pl.BlockSpec((1, bq, D_A), lambda b_, i: (b_, i, 0)),
                  pl.BlockSpec((1, t, D_A), lambda b_, i: (b_, 0, 0)),
                  pl.BlockSpec((1, nb, D_A, bq), lambda b_, i: (b_, 0, 0, 0)),
                  pl.BlockSpec((1, nb, D_A), lambda b_, i: (b_, 0, 0)),
                  pl.BlockSpec((nb, bq, LANES), lambda b_, i: (0, 0, 0))],
        out_specs=pl.BlockSpec((1, bq, D_A), lambda b_, i: (b_, i, 0)),
        out_shape=jax.ShapeDtypeStruct((b, t, D_A), jnp.float32),
        scratch_shapes=[pltpu.VMEM((N_HEADS_A, bq, 2 * LANES), jnp.bfloat16),
                        pltpu.VMEM((N_HEADS_A, nb, bq), jnp.float32),
                        pltpu.VMEM((N_HEADS_A, 1, bq), jnp.float32),
                        pltpu.VMEM((N_HEADS_A, 1, bq), jnp.float32),
                        pltpu.VMEM((N_HEADS_A, HEAD_DIM_A, bq), jnp.float32),
                        pltpu.VMEM((2 * N_HEADS_A, bq, bq), jnp.float32)],
        compiler_params=pltpu.CompilerParams(dimension_semantics=("arbitrary", "arbitrary"),
                                             vmem_limit_bytes=VMEM_LIMIT),
        name="moba_prompt",
    )(qs, qf, kbf, vtb, means, onehot)


def _pages_copy(cache_ref, layer, page, head0, buf, slot, pg, sem):
    hg = buf.shape[1]
    return pltpu.make_async_copy(cache_ref.at[layer, page, pl.ds(head0, hg)],
                                 buf.at[slot, :, :, pl.ds(pg * LANES, LANES)], sem)


def _moba_sample_kernel(layer, nseq, n_pages, pt_ref, q_ref, kn_ref, vn_ref, ck_ref, cv_ref, o_ref,
                        kbuf, vbuf, sem, s_ref, p_ref):
    b, g = pl.program_id(0), pl.program_id(1)
    ngrp = pl.num_programs(1)
    hg, hd = kbuf.shape[1], kbuf.shape[2]
    nkey = kbuf.shape[3]
    nblk = nkey // MOBA_BLOCK
    ppb = MOBA_BLOCK // LANES
    step = b * ngrp + g
    last = nseq * ngrp - 1
    nslot = kbuf.shape[0]
    ahead = nslot - 1
    slot = step % nslot

    def issue(bb, gg, sl, pg):
        page = pt_ref[bb * n_pages + pg]
        _pages_copy(ck_ref, layer, page, gg * hg, kbuf, sl, pg, sem.at[0, sl]).start()
        _pages_copy(cv_ref, layer, page, gg * hg, vbuf, sl, pg, sem.at[1, sl]).start()

    def wait_all(sl):
        for pg in range(n_pages):
            _pages_copy(ck_ref, layer, 0, 0, kbuf, sl, pg, sem.at[0, sl]).wait()
        for pg in range(n_pages):
            _pages_copy(cv_ref, layer, 0, 0, vbuf, sl, pg, sem.at[1, sl]).wait()

    @pl.when(step == 0)
    def _():
        for d in range(ahead):
            first = jnp.minimum(d, last)
            for pg in range(n_pages):
                issue(first // ngrp, first % ngrp, d, pg)

    wait_all(slot)
    nxt = jnp.minimum(step + ahead, last)
    nxt_b, nxt_g = nxt // ngrp, nxt % ngrp
    nxt_slot = (step + ahead) % nslot

    ts = q_ref.shape[2]
    lane = lax.broadcasted_iota(jnp.int32, (ts, LANES), 1)
    trow = lax.broadcasted_iota(jnp.int32, (ts, LANES), 0)
    mlane = lax.broadcasted_iota(jnp.int32, (hd, LANES), 1)
    pad = jnp.zeros((LANES - ts, hd), jnp.float32)
    heads = range(hg)
    mts = []
    for h in heads:
        mt = jnp.zeros((hd, LANES), jnp.float32)
        for j in range(nblk):
            x = kbuf[slot, h, :, j * MOBA_BLOCK:j * MOBA_BLOCK + LANES]
            for pp in range(1, ppb):
                x = x + kbuf[slot, h, :, j * MOBA_BLOCK + pp * LANES:j * MOBA_BLOCK + (pp + 1) * LANES]
            mt = jnp.where(mlane == j, jnp.sum(x, axis=1, keepdims=True), mt)
            if h == 0:
                for pp in range(ppb):
                    issue(nxt_b, nxt_g, nxt_slot, j * ppb + pp)
        mts.append(mt * (1.0 / MOBA_BLOCK))
    qs = [q_ref[0, h] for h in heads]
    sbs = [jnp.where(lane < nblk,
                     jnp.dot(qs[h], mts[h], precision=_HI, preferred_element_type=jnp.float32), NEG)
           for h in heads]
    qbs = [_bf(q * (hd ** -0.5)) for q in qs]
    for h in heads:
        s_ref[h] = jnp.dot(qbs[h], _bf(kbuf[slot, h]), preferred_element_type=jnp.float32)
    sels = []
    for h in heads:
        rank = jnp.zeros((ts, LANES), jnp.float32)
        for jp in range(nblk):
            col = sbs[h][:, jp:jp + 1]
            before = (col > sbs[h]) | ((col == sbs[h]) & (lane > jp))
            rank = rank + jnp.where(before, 1.0, 0.0)
        sels.append(jnp.where((rank < MOBA_TOPK) & (lane < nblk), 1.0, 0.0))
    vns, pos, ls = [], [], []
    for h in heads:
        s = jnp.concatenate(
            [jnp.where(sels[h][:, j:j + 1] > 0.0, s_ref[h, :, j * MOBA_BLOCK:(j + 1) * MOBA_BLOCK],
                       NEG) for j in range(nblk)], axis=1)
        kn = _bf(jnp.concatenate([kn_ref[0, h], pad], axis=0))
        vns.append(_bf(jnp.concatenate([vn_ref[0, h], pad], axis=0)))
        so = lax.dot_general(qbs[h], kn, _NT, preferred_element_type=jnp.float32)
        so = jnp.where(lane <= trow, so, NEG)
        m = jnp.maximum(jnp.max(s, axis=1, keepdims=True), jnp.max(so, axis=1, keepdims=True))
        p = jnp.exp(s - m)
        po = jnp.exp(so - m)
        ls.append(jnp.sum(p, axis=1, keepdims=True) + jnp.sum(po, axis=1, keepdims=True))
        p_ref[h] = _bf(p)
        pos.append(_bf(po))
    for h in heads:
        o = (lax.dot_general(p_ref[h], _bf(vbuf[slot, h]), _NT, preferred_element_type=jnp.float32)
             + jnp.dot(pos[h], vns[h], preferred_element_type=jnp.float32))
        o_ref[0, h] = o / ls[h]

    @pl.when(step == last)
    def _():
        for d in range(1, nslot):
            wait_all((step + d) % nslot)


def _moba_sample(layer, page_table, q_hm, k_hm, v_hm, cache_kt, cache_vt):
    db, nh, ts, hd = q_hm.shape
    n_pages = page_table.shape[1]
    page = cache_kt.shape[4]
    hg = SAMPLE_HEAD_GROUP
    assert page == LANES and MOBA_BLOCK % page == 0 and nh % hg == 0
    assert (n_pages * page) % MOBA_BLOCK == 0 and n_pages * page // MOBA_BLOCK <= LANES
    blk = pl.BlockSpec((1, hg, ts, hd), lambda b, g, pt: (b, g, 0, 0))
    grid_spec = pltpu.PrefetchScalarGridSpec(
        num_scalar_prefetch=1,
        grid=(db, nh // hg),
        in_specs=[blk, blk, blk, pl.BlockSpec(memory_space=pl.ANY), pl.BlockSpec(memory_space=pl.ANY)],
        out_specs=blk,
        scratch_shapes=[pltpu.VMEM((SAMPLE_SLOTS, hg, hd, n_pages * page), jnp.float32),
                        pltpu.VMEM((SAMPLE_SLOTS, hg, hd, n_pages * page), jnp.float32),
                        pltpu.SemaphoreType.DMA((2, SAMPLE_SLOTS)),
                        pltpu.VMEM((hg, ts, n_pages * page), jnp.float32),
                        pltpu.VMEM((hg, ts, n_pages * page), jnp.bfloat16)])
    return pl.pallas_call(
        functools.partial(_moba_sample_kernel, layer, db, n_pages),
        grid_spec=grid_spec,
        out_shape=jax.ShapeDtypeStruct((db, nh, ts, hd), jnp.float32),
        compiler_params=pltpu.CompilerParams(dimension_semantics=("arbitrary", "arbitrary"),
                                             vmem_limit_bytes=VMEM_LIMIT),
        name="moba_sample",
    )(page_table.reshape(-1), q_hm, k_hm, v_hm, cache_kt, cache_vt)


def _gla_chunks(c, qs, ks, gs, vs, sts, seg, ltri, hmask, trow):
    dk, dv = DK_B, DV_B
    n = len(qs)
    heads = range(N_HEADS_B)
    f32 = jnp.float32
    zero = jnp.zeros((c, D_BK), f32)
    cums = [jnp.dot(ltri, g, precision=_HI, preferred_element_type=f32) for g in gs]
    pmats = []
    for q, k, cum in zip(qs, ks, cums):
        parts = []
        for s in range(c):
            e = jnp.exp(jnp.where(trow >= s, cum - cum[s:s + 1, :], NEG))
            parts.append(q * k[s:s + 1, :] * e)
        pmats.append(_bf(jnp.concatenate(parts, axis=0)))
    rs = [jnp.dot(pm, seg, preferred_element_type=f32) for pm in pmats]
    by_head = lambda x: _bf(jnp.concatenate([jnp.where(hmask[h], x, zero) for h in heads], axis=0))
    qes = [by_head(q * jnp.exp(cum)) for q, cum in zip(qs, cums)]
    kds = [by_head(k * jnp.exp(cum[c - 1:c, :] - cum)) for k, cum in zip(ks, cums)]
    inters = [lax.dot_general(qes[i], _bf(sts[i]), _NT, preferred_element_type=f32)
              for i in range(n)]
    vstk = [_bf(jnp.concatenate([v[:, h * dv:(h + 1) * dv] for h in heads], axis=0)) for v in vs]
    upds = [lax.dot_general(vstk[i], kds[i], _TN, preferred_element_type=f32) for i in range(n)]
    outs, new = [], []
    for i in range(n):
        r, v = rs[i], vs[i]
        o = r[0:c] * v[0:1, :]
        for s in range(1, c):
            o = o + r[s * c:(s + 1) * c] * v[s:s + 1, :]
        outs.append(o + jnp.concatenate([inters[i][h * c:(h + 1) * c] for h in heads], axis=1))
        new.append(sts[i] * jnp.exp(cums[i][c - 1:c, :]) + upds[i])
    return outs, new


def _gla_kernel(chunk, q_ref, k_ref, g_ref, v_ref, s0_ref, seg_ref, o_ref, sout_ref, s_ref):
    tb = pl.program_id(1)
    c = chunk
    nsb = q_ref.shape[0]
    nchunk = q_ref.shape[1] // c

    @pl.when(tb == 0)
    def _():
        for sq in range(nsb):
            s_ref[sq] = s0_ref[sq].T

    rows = lax.broadcasted_iota(jnp.int32, (c, c), 0)
    cols = lax.broadcasted_iota(jnp.int32, (c, c), 1)
    ltri = jnp.where(rows >= cols, 1.0, 0.0)
    trow = lax.broadcasted_iota(jnp.int32, (c, D_BK), 0)
    lane = lax.broadcasted_iota(jnp.int32, (c, D_BK), 1)
    hmask = [(lane >= h * DK_B) & (lane < (h + 1) * DK_B) for h in range(N_HEADS_B)]

    def body(ci, carry):
        t0 = pl.multiple_of(ci * c, c)
        tok = lambda ref: [ref[sq, pl.ds(t0, c), :] for sq in range(nsb)]
        outs, new = _gla_chunks(c, tok(q_ref), tok(k_ref), tok(g_ref), tok(v_ref),
                                [s_ref[sq] for sq in range(nsb)], seg_ref[...], ltri, hmask, trow)
        for sq in range(nsb):
            o_ref[sq, pl.ds(t0, c), :] = outs[sq]
            s_ref[sq] = new[sq]
        return carry

    lax.fori_loop(0, nchunk, body, 0)

    @pl.when(tb == pl.num_programs(1) - 1)
    def _():
        for sq in range(nsb):
            sout_ref[sq] = s_ref[sq].T


def _gla(qb, kb, gk, vb, s0, seg, *, chunk, tblock, nsb):
    nseq, t, _ = qb.shape
    assert t % tblock == 0 and tblock % chunk == 0 and nseq % nsb == 0
    ntb = t // tblock
    tok = lambda w: pl.BlockSpec((nsb, tblock, w), lambda s, tb: (s, tb, 0))
    st_spec = pl.BlockSpec((nsb, D_BK, DV_B), lambda s, tb: (s, 0, 0))
    return pl.pallas_call(
        functools.partial(_gla_kernel, chunk),
        grid=(nseq // nsb, ntb),
        in_specs=[tok(D_BK), tok(D_BK), tok(D_BK), tok(D_BV), st_spec,
                  pl.BlockSpec(seg.shape, lambda s, tb: (0, 0))],
        out_specs=[tok(D_BV), st_spec],
        out_shape=[jax.ShapeDtypeStruct((nseq, t, D_BV), jnp.float32),
                   jax.ShapeDtypeStruct((nseq, D_BK, DV_B), jnp.float32)],
        scratch_shapes=[pltpu.VMEM((nsb, DV_B, D_BK), jnp.float32)],
        compiler_params=pltpu.CompilerParams(dimension_semantics=("arbitrary", "arbitrary"),
                                             vmem_limit_bytes=VMEM_LIMIT),
        name="gla_c%d" % chunk,
    )(qb, kb, gk, vb, s0, seg)


def _post_kernel(x_ref, oa_ref, ob_ref, gb_ref, gn_ref, wo_ref, nm_ref, wu_ref, wd_ref, nf_ref,
                 y_ref, x1_ref, h2_ref, acc_ref):
    kf = pl.program_id(1)

    @pl.when(kf == 0)
    def _():
        ob = ob_ref[...]
        gb = gb_ref[...]
        gn = gn_ref[...]
        heads = []
        for h in range(N_HEADS_B):
            sl = slice(h * DV_B, (h + 1) * DV_B)
            gate = gb[:, sl]
            heads.append(_rms(ob[:, sl], gn[:, sl]) * (gate * (1.0 / (1.0 + jnp.exp(-gate)))))
        mix = _bf(jnp.concatenate([oa_ref[...]] + heads, axis=1))
        x1 = x_ref[...] + jnp.dot(mix, wo_ref[...], preferred_element_type=jnp.float32)
        x1_ref[...] = x1
        h2_ref[...] = _bf(_rms(x1, nm_ref[...]))
        acc_ref[...] = jnp.zeros_like(acc_ref)

    u = jnp.maximum(jnp.dot(h2_ref[...], wu_ref[...], preferred_element_type=jnp.float32), 0.0)
    acc_ref[...] += jnp.dot(_bf(u * u), wd_ref[...], preferred_element_type=jnp.float32)

    @pl.when(kf == pl.num_programs(1) - 1)
    def _():
        y_ref[...] = _rms(x1_ref[...] + acc_ref[...], nf_ref[...])


def _post(x2d, oa, ob, gb, gla_norm, w_o, norm_mlp, w_up, w_down, norm_final):
    n = x2d.shape[0]
    tm = POST_TILE if n % POST_TILE == 0 else TOKEN_TILE
    tf = FF_TILE
    assert n % tm == 0 and D_FF % tf == 0
    row = lambda w: pl.BlockSpec((tm, w), lambda i, kf: (i, 0))
    full = lambda a: pl.BlockSpec(a.shape, lambda i, kf: (0,) * a.ndim)
    return pl.pallas_call(
        _post_kernel,
        grid=(n // tm, D_FF // tf),
        in_specs=[row(D_MODEL), row(D_A), row(D_BV), row(D_BV), full(gla_norm), full(w_o),
                  full(norm_mlp),
                  pl.BlockSpec((D_MODEL, tf), lambda i, kf: (0, kf)),
                  pl.BlockSpec((tf, D_MODEL), lambda i, kf: (kf, 0)),
                  full(norm_final)],
        out_specs=row(D_MODEL),
        out_shape=jax.ShapeDtypeStruct((n, D_MODEL), jnp.float32),
        scratch_shapes=[pltpu.VMEM((tm, D_MODEL), jnp.float32),
                        pltpu.VMEM((tm, D_MODEL), jnp.bfloat16),
                        pltpu.VMEM((tm, D_MODEL), jnp.float32)],
        compiler_params=pltpu.CompilerParams(dimension_semantics=("arbitrary", "arbitrary"),
                                             vmem_limit_bytes=VMEM_LIMIT),
        name="post",
    )(x2d, oa, ob, gb, gla_norm, w_o, norm_mlp, w_up, w_down, norm_final)


def _seg_matrix():
    d = np.arange(D_BK)[:, None] // DK_B
    e = np.arange(D_BV)[None, :] // DV_B
    return jnp.asarray(d == e, jnp.bfloat16)


def _layer(l, x_prompt, x_sample, cache_k, cache_v, state_gla, page_table, norm_mix, w_in, w_gk2,
           b_gk, gla_norm, w_o, norm_mlp, w_up, w_down):
    b, t_p, _ = x_prompt.shape
    db, t_s, _ = x_sample.shape
    page = cache_k.shape[3]
    n_pages = page_table.shape[1]
    past = n_pages * page
    assert past % MOBA_BLOCK == 0 and past // MOBA_BLOCK >= MOBA_TOPK
    assert t_s <= 8 and TOKEN_TILE % t_s == 0

    w_in_p = _bf(jnp.concatenate(
        [w_in[l], jnp.zeros((D_MODEL, D_IN_PAD - D_IN), w_in.dtype)], axis=1))
    w_gk2_p = _bf(jnp.concatenate(
        [w_gk2[l], jnp.zeros((LANES - GATE_RANK, D_BK), w_gk2.dtype)], axis=0))
    nmix = norm_mix[l].reshape(1, D_MODEL)
    bg = b_gk[l].reshape(1, D_BK)
    gn = gla_norm[l].reshape(1, D_BV)
    nmlp = norm_mlp[l].reshape(1, D_MODEL)
    wo, wu, wd = _bf(w_o[l]), _bf(w_up[l]), _bf(w_down[l])
    seg = _seg_matrix()

    xp2 = x_prompt.reshape(b * t_p, D_MODEL)
    tab_p = _rope_tables(jnp.arange(t_p, dtype=jnp.int32))
    (qs, qf, kbf, vtb, kt_p, vt_p, means, qb, kb, vb, gb, gk) = _in_proj(
        xp2, nmix, w_in_p, w_gk2_p, bg, tab_p, sample=False, nseq=b, seq_len=t_p)
    nb = t_p // MOBA_BLOCK
    oa_p = _moba_prompt(qs.reshape(b, t_p, D_A), qf.reshape(b, t_p, D_A), kbf.reshape(b, t_p, D_A),
                        vtb, means.reshape(b, nb, D_A))
    s0_p = jnp.zeros((b, D_BK, DV_B), state_gla.dtype)
    ob_p, s_p = _gla(qb.reshape(b, t_p, D_BK), kb.reshape(b, t_p, D_BK), gk.reshape(b, t_p, D_BK),
                     vb.reshape(b, t_p, D_BV), s0_p, seg, chunk=GLA_CHUNK_PROMPT,
                     tblock=min(GLA_TBLOCK, t_p), nsb=b)
    prompt = (xp2, oa_p.reshape(b * t_p, D_A), ob_p.reshape(b * t_p, D_BV), gb)
    to_rows = lambda a: jnp.swapaxes(a.reshape(b, N_HEADS_A, HEAD_DIM_A, t_p), 2, 3)
    k_p, v_p = to_rows(kt_p), to_rows(vt_p)

    xs2 = x_sample.reshape(db * t_s, D_MODEL)
    pos_s = past + jnp.tile(jnp.arange(t_s, dtype=jnp.int32), db)
    tab_s = _rope_tables(pos_s)
    (q_s, k_s, v_s, qb_s, kb_s, vb_s, gb_s, gk_s) = _in_proj(
        xs2, nmix, w_in_p, w_gk2_p, bg, tab_s, sample=True, nseq=db, seq_len=t_s)
    oa_s = _moba_sample(l, page_table, q_s, k_s, v_s,
                        jnp.swapaxes(cache_k, 3, 4), jnp.swapaxes(cache_v, 3, 4))
    oa_s = oa_s.transpose(0, 2, 1, 3).reshape(db * t_s, D_A)
    nsb = GLA_SEQS_PER_STEP if db % GLA_SEQS_PER_STEP == 0 else 1
    ob_s, s_s = _gla(qb_s.reshape(db, t_s, D_BK), kb_s.reshape(db, t_s, D_BK),
                     gk_s.reshape(db, t_s, D_BK), vb_s.reshape(db, t_s, D_BV),
                     state_gla[l].reshape(db, D_BK, DV_B), seg, chunk=t_s, tblock=t_s, nsb=nsb)
    sample = (xs2, oa_s, ob_s.reshape(db * t_s, D_BV), gb_s)
    return prompt, sample, (gn, wo, nmlp, wu, wd), (k_p, v_p, s_p, k_s, v_s, s_s)


def kernel(x_prompt, x_sample, cache_k, cache_v, state_gla, page_table, norm_mix, w_in, w_gk2, b_gk,
           gla_norm, w_o, norm_mlp, w_up, w_down, norm_final):
    depth = norm_mix.shape[0]
    assert depth == 1, "the trunk is one layer deep"
    b, t_p, _ = x_prompt.shape
    db, t_s, _ = x_sample.shape
    prompt, sample, weights, new = _layer(0, x_prompt, x_sample, cache_k, cache_v, state_gla,
                                          page_table, norm_mix, w_in, w_gk2, b_gk, gla_norm, w_o,
                                          norm_mlp, w_up, w_down)
    nf = norm_final.reshape(1, D_MODEL)
    y_p = _post(*prompt, *weights, nf).reshape(b, t_p, D_MODEL)
    y_s = _post(*sample, *weights, nf).reshape(db, t_s, D_MODEL)
    k_p, v_p, s_p, k_s, v_s, s_s = new
    st = lambda s: s.reshape(1, s.shape[0], N_HEADS_B, DK_B, DV_B)
    return (y_p, y_s, k_p[None], v_p[None], st(s_p), k_s[None], v_s[None], st(s_s))
```

```python
import functools

import jax
import jax.numpy as jnp
import numpy as np
from jax import lax
from jax.experimental import pallas as pl
from jax.experimental.pallas import tpu as pltpu

D_MODEL = 1024
HEAD_DIM_A = 64
D_A = D_MODEL // 2
N_HEADS_A = D_A // HEAD_DIM_A
MOBA_BLOCK = 256
MOBA_TOPK = 3
ROPE_DIM = HEAD_DIM_A // 4
ROPE_THETA = 500000.0
D_BV = D_MODEL // 2
N_HEADS_B = 4
DV_B = D_BV // N_HEADS_B
DK_B = DV_B // 2
D_BK = N_HEADS_B * DK_B
GATE_RANK = 16
GATE_NORMALIZER = 16.0
D_IN = 3 * D_A + 2 * D_BK + 2 * D_BV + GATE_RANK
D_FF = 4 * D_MODEL
EPS = 1e-6
NEG = -1e30
LOG2_E = 1.4426950408889634

LANES = 128
D_IN_PAD = D_IN - GATE_RANK + LANES
_OFF_QA, _OFF_KA, _OFF_VA = 0, D_A, 2 * D_A
_OFF_QB = 3 * D_A
_OFF_KB = _OFF_QB + D_BK
_OFF_VB = _OFF_KB + D_BK
_OFF_GB = _OFF_VB + D_BV
_OFF_LR = _OFF_GB + D_BV

TOKEN_TILE = 256
POST_TILE = 512
GLA_CHUNK_PROMPT = 16
GLA_TBLOCK = 256
GLA_SEQS_PER_STEP = 8
FF_TILE = 1024
SAMPLE_HEAD_GROUP = 2
SAMPLE_SLOTS = 3
VMEM_LIMIT = 48 * 1024 * 1024

_HI = lax.Precision.HIGHEST
_NT = (((1,), (1,)), ((), ()))
_TN = (((0,), (0,)), ((), ()))


def _bf(x):
    return x.astype(jnp.bfloat16)


def _rms(x, g):
    r = lax.rsqrt(jnp.mean(x * x, axis=-1, keepdims=True) + EPS)
    return (x * r) * g


def _rope(z, cos, sin_lo, sin_hi):
    n = z.shape[1]
    cos = jnp.concatenate([cos] * (n // LANES), axis=1)
    sin_lo = jnp.concatenate([sin_lo] * (n // LANES), axis=1)
    sin_hi = jnp.concatenate([sin_hi] * (n // LANES), axis=1)
    half = ROPE_DIM // 2
    up = pltpu.roll(z, n - half, 1)
    dn = pltpu.roll(z, half, 1)
    return z * cos + up * sin_lo + dn * sin_hi


def _in_proj_kernel(sample, x_ref, g_ref, w_ref, wg_ref, bg_ref, cos_ref, slo_ref, shi_ref, *outs):
    tm = x_ref.shape[0]
    hb = _bf(_rms(x_ref[...], g_ref[...]))

    def proj(off, width):
        return jnp.dot(hb, w_ref[:, off:off + width], preferred_element_type=jnp.float32)

    cos, slo, shi = cos_ref[...], slo_ref[...], shi_ref[...]
    q = _rope(proj(_OFF_QA, D_A), cos, slo, shi)
    k = _rope(proj(_OFF_KA, D_A), cos, slo, shi)
    v = proj(_OFF_VA, D_A)
    if sample:
        (q_hm, k_hm, v_hm, qb_o, kb_o, vb_o, gb_o, gk_o) = outs
        nseq = q_hm.shape[0]
        for h in range(N_HEADS_A):
            sl = slice(h * HEAD_DIM_A, (h + 1) * HEAD_DIM_A)
            q_hm[:, h] = q[:, sl].reshape(nseq, tm // nseq, HEAD_DIM_A)
            k_hm[:, h] = k[:, sl].reshape(nseq, tm // nseq, HEAD_DIM_A)
            v_hm[:, h] = v[:, sl].reshape(nseq, tm // nseq, HEAD_DIM_A)
    else:
        (qs_o, qf_o, kbf_o, vtb_o, kt_o, vt_o, mean_o, qb_o, kb_o, vb_o, gb_o, gk_o) = outs
        qs_o[...] = _bf(q * (HEAD_DIM_A ** -0.5 * LOG2_E))
        qf_o[...] = q
        kbf_o[...] = _bf(k)
        vt = v.T
        kt_o[0] = k.T
        vt_o[0] = vt
        vtb_o[0, 0] = _bf(vt)
        mean_o[0, 0] = jnp.sum(k, axis=0, keepdims=True) * (1.0 / tm)
    qb_o[...] = proj(_OFF_QB, D_BK) * (DK_B ** -0.5)
    kb_o[...] = proj(_OFF_KB, D_BK)
    vb_o[...] = proj(_OFF_VB, D_BV)
    gb_o[...] = proj(_OFF_GB, D_BV)
    lr = _bf(proj(_OFF_LR, LANES))
    zg = jnp.dot(lr, wg_ref[...], preferred_element_type=jnp.float32) + bg_ref[...]
    log_sig = jnp.minimum(zg, 0.0) - jnp.log1p(jnp.exp(-jnp.abs(zg)))
    gk_o[...] = log_sig * (1.0 / GATE_NORMALIZER)


def _in_proj(x2d, norm_mix, w_in_p, w_gk2_p, b_gk, tables, *, sample, nseq, seq_len):
    n = x2d.shape[0]
    tm = TOKEN_TILE
    assert n % tm == 0
    nt = n // tm
    cos, slo, shi = tables
    f32 = jnp.float32
    row = lambda w: pl.BlockSpec((tm, w), lambda i: (i, 0))
    full = lambda a: pl.BlockSpec(a.shape, lambda i: (0,) * a.ndim)
    gla_shapes = [jax.ShapeDtypeStruct((n, D_BK), f32), jax.ShapeDtypeStruct((n, D_BK), f32),
                  jax.ShapeDtypeStruct((n, D_BV), f32), jax.ShapeDtypeStruct((n, D_BV), f32),
                  jax.ShapeDtypeStruct((n, D_BK), f32)]
    gla_specs = [row(D_BK), row(D_BK), row(D_BV), row(D_BV), row(D_BK)]
    if sample:
        assert tm % seq_len == 0
        spt = tm // seq_len
        hm = jax.ShapeDtypeStruct((nseq, N_HEADS_A, seq_len, HEAD_DIM_A), f32)
        hm_spec = pl.BlockSpec((spt, N_HEADS_A, seq_len, HEAD_DIM_A), lambda i: (i, 0, 0, 0))
        out_shape = [hm, hm, hm] + gla_shapes
        out_specs = [hm_spec, hm_spec, hm_spec] + gla_specs
        tab_spec = row(LANES)
    else:
        assert seq_len % tm == 0 and tm == MOBA_BLOCK
        tps = seq_len // tm
        nb = tps
        chan_major = jax.ShapeDtypeStruct((nseq, D_A, seq_len), f32)
        chan_spec = pl.BlockSpec((1, D_A, tm), lambda i: (i // tps, 0, i % tps))
        out_shape = [jax.ShapeDtypeStruct((n, D_A), jnp.bfloat16),
                     jax.ShapeDtypeStruct((n, D_A), f32),
                     jax.ShapeDtypeStruct((n, D_A), jnp.bfloat16),
                     jax.ShapeDtypeStruct((nseq, nb, D_A, tm), jnp.bfloat16),
                     chan_major, chan_major,
                     jax.ShapeDtypeStruct((nseq, nb, 1, D_A), f32)] + gla_shapes
        out_specs = [row(D_A), row(D_A), row(D_A),
                     pl.BlockSpec((1, 1, D_A, tm), lambda i: (i // tps, i % tps, 0, 0)),
                     chan_spec, chan_spec,
                     pl.BlockSpec((1, 1, 1, D_A), lambda i: (i // tps, i % tps, 0, 0))] + gla_specs
        tab_spec = pl.BlockSpec((tm, LANES), lambda i: (i % tps, 0))
    return pl.pallas_call(
        functools.partial(_in_proj_kernel, sample),
        grid=(nt,),
        in_specs=[row(D_MODEL), full(norm_mix), full(w_in_p), full(w_gk2_p), full(b_gk),
                  tab_spec, tab_spec, tab_spec],
        out_specs=out_specs,
        out_shape=out_shape,
        compiler_params=pltpu.CompilerParams(dimension_semantics=("arbitrary",),
                                             vmem_limit_bytes=VMEM_LIMIT),
        name="in_proj_sample" if sample else "in_proj_prompt",
    )(x2d, norm_mix, w_in_p, w_gk2_p, b_gk, cos, slo, shi)


def _rope_tables(pos):
    half = ROPE_DIM // 2
    d = lax.broadcasted_iota(jnp.int32, (1, LANES), 1) % HEAD_DIM_A
    inv = ROPE_THETA ** (-(d % half).astype(jnp.float32) / half)
    ang = pos.astype(jnp.float32)[:, None] * inv
    c, s = jnp.cos(ang), jnp.sin(ang)
    cos = jnp.where(d < ROPE_DIM, c, 1.0)
    sin_lo = jnp.where(d < half, -s, 0.0)
    sin_hi = jnp.where((d >= half) & (d < ROPE_DIM), s, 0.0)
    return cos, sin_lo, sin_hi


def _rank_rows(s, blk, nb):
    rank = jnp.zeros(s.shape, jnp.float32)
    for jp in range(nb):
        row = s[jp:jp + 1, :]
        before = (row > s) | ((row == s) & (blk > jp))
        rank = rank + jnp.where(before, 1.0, 0.0)
    return rank


def _moba_kernel(qs_ref, qf_ref, k_ref, vt_ref, mean_ref, oh_ref, o_ref, qm_ref, sel_ref, m_ref,
                 l_ref, acc_ref, s_ref):
    i = pl.program_id(1)
    bq = qs_ref.shape[1]
    nb = mean_ref.shape[1]
    hd = HEAD_DIM_A
    nh = N_HEADS_A
    lane = lax.broadcasted_iota(jnp.int32, (bq, LANES), 1)
    halves = (lane < hd, lane >= hd)
    blk = lax.broadcasted_iota(jnp.int32, (nb, bq), 0)
    pair = lambda h: slice((h // 2) * LANES, (h // 2 + 1) * LANES)

    row0 = pl.multiple_of(i * bq, bq)
    keep = [_bf(jnp.where(hm, 1.0, 0.0)) for hm in halves]
    for h in range(nh):
        qm_ref[h, :, 0:LANES] = qs_ref[0, :, pair(h)] * keep[h % 2]
    kidx = lax.broadcasted_iota(jnp.int32, (bq, bq), 0)
    qidx = lax.broadcasted_iota(jnp.int32, (bq, bq), 1)
    tops = []
    for h in range(nh):
        kd = k_ref[0, pl.ds(row0, bq), pair(h)]
        s = lax.dot_general(kd, qm_ref[h, :, 0:LANES], _NT, preferred_element_type=jnp.float32)
        s = jnp.where(kidx <= qidx, s, NEG)
        s_ref[h] = s
        tops.append(jnp.max(s, axis=0, keepdims=True))
    mlane = lax.broadcasted_iota(jnp.int32, (nb, LANES), 1)
    for h in range(nh):
        mh = jnp.where((mlane >= hd) == bool(h % 2), mean_ref[0, :, pair(h)], 0.0)
        sel_ref[h] = lax.dot_general(mh, qf_ref[0, :, pair(h)], _NT, precision=_HI,
                                     preferred_element_type=jnp.float32)
    for h in range(nh):
        s = jnp.where(blk < i, sel_ref[h], NEG)
        rank = _rank_rows(s, blk, nb)
        bias = jnp.where((rank < MOBA_TOPK) & (blk < i), 0.0, NEG)
        bias = jnp.concatenate([bias, jnp.zeros((LANES - nb, bq), jnp.float32)], axis=0)
        qm_ref[h, :, LANES:2 * LANES] = _bf(bias.T)

    for h in range(nh):
        p = jnp.exp2(s_ref[h] - tops[h])
        m_ref[h] = tops[h]
        l_ref[h] = jnp.sum(p, axis=0, keepdims=True)
        acc_ref[h] = jnp.dot(vt_ref[0, i, h * hd:(h + 1) * hd, :], _bf(p),
                             preferred_element_type=jnp.float32)

    def past(js):
        tops = []
        for h in range(nh):
            top = None
            for n, j in enumerate(js):
                rowj = pl.multiple_of(j * bq, bq)
                kj = jnp.concatenate([k_ref[0, pl.ds(rowj, bq), pair(h)], oh_ref[j]], axis=1)
                s = lax.dot_general(kj, qm_ref[h], _NT, preferred_element_type=jnp.float32)
                s_ref[n * nh + h] = s
                t = jnp.max(s, axis=0, keepdims=True)
                top = t if top is None else jnp.maximum(top, t)
            tops.append(top)
        for h in range(nh):
            m_old = m_ref[h]
            m_new = jnp.maximum(m_old, tops[h])
            alpha = jnp.exp2(m_old - m_new)
            l_new = alpha * l_ref[h]
            acc = alpha * acc_ref[h]
            for n, j in enumerate(js):
                p = jnp.exp2(s_ref[n * nh + h] - m_new)
                l_new = l_new + jnp.sum(p, axis=0, keepdims=True)
                acc = acc + jnp.dot(vt_ref[0, j, h * hd:(h + 1) * hd, :], _bf(p),
                                    preferred_element_type=jnp.float32)
            m_ref[h] = m_new
            l_ref[h] = l_new
            acc_ref[h] = acc

    def past_pair(t, carry):
        past([2 * t, 2 * t + 1])
        return carry

    lax.fori_loop(0, i // 2, past_pair, 0)

    @pl.when(i % 2 == 1)
    def _():
        past([i - 1])

    for p2 in range(nh // 2):
        ot = jnp.concatenate([acc_ref[2 * p2] / l_ref[2 * p2],
                              acc_ref[2 * p2 + 1] / l_ref[2 * p2 + 1]], axis=0)
        o_ref[0, :, p2 * LANES:(p2 + 1) * LANES] = ot.T


def _moba_prompt(qs, qf, kbf, vtb, means):
    b, t, _ = qs.shape
    nb = t // MOBA_BLOCK
    bq = MOBA_BLOCK
    assert nb <= LANES
    onehot = jnp.asarray(np.arange(LANES)[None, None, :] == np.arange(nb)[:, None, None],
                         jnp.bfloat16) * jnp.ones((nb, bq, LANES), jnp.bfloat16)
    return pl.pallas_call(
        _moba_kernel,
        grid=(b, nb),
        in_specs=[pl.BlockSpec((1, bq, D_A), lambda b_, i: (b_, i, 0)),
                  pl.BlockSpec((1, bq, D_A), lambda b_, i: (b_, i, 0)),
                  pl.BlockSpec((1, t, D_A), lambda b_, i: (b_, 0, 0)),
                  pl.BlockSpec((1, nb, D_A, bq), lambda b_, i: (b_, 0, 0, 0)),
                  pl.BlockSpec((1, nb, D_A), lambda b_, i: (b_, 0, 0)),
                  pl.BlockSpec((nb, bq, LANES), lambda b_, i: (0, 0, 0))],
        out_specs=pl.BlockSpec((1, bq, D_A), lambda b_, i: (b_, i, 0)),
        out_shape=jax.ShapeDtypeStruct((b, t, D_A), jnp.float32),
        scratch_shapes=[pltpu.VMEM((N_HEADS_A, bq, 2 * LANES), jnp.bfloat16),
                        pltpu.VMEM((N_HEADS_A, nb, bq), jnp.float32),
                        pltpu.VMEM((N_HEADS_A, 1, bq), jnp.float32),
                        pltpu.VMEM((N_HEADS_A, 1, bq), jnp.float32),
                        pltpu.VMEM((N_HEADS_A, HEAD_DIM_A, bq), jnp.float32),
                        pltpu.VMEM((2 * N_HEADS_A, bq, bq), jnp.float32)],
        compiler_params=pltpu.CompilerParams(dimension_semantics=("arbitrary", "arbitrary"),
                                             vmem_limit_bytes=VMEM_LIMIT),
        name="moba_prompt",
    )(qs, qf, kbf, vtb, means, onehot)


def _pages_copy(cache_ref, layer, page, head0, buf, slot, pg, sem):
    hg = buf.shape[1]
    return pltpu.make_async_copy(cache_ref.at[layer, page, pl.ds(head0, hg)],
                                 buf.at[slot, :, :, pl.ds(pg * LANES, LANES)], sem)


def _moba_sample_kernel(layer, nseq, n_pages, pt_ref, q_ref, kn_ref, vn_ref, ck_ref, cv_ref, o_ref,
                        kbuf, vbuf, sem, s_ref, p_ref):
    b, g = pl.program_id(0), pl.program_id(1)
    ngrp = pl.num_programs(1)
    hg, hd = kbuf.shape[1], kbuf.shape[2]
    nkey = kbuf.shape[3]
    nblk = nkey // MOBA_BLOCK
    ppb = MOBA_BLOCK // LANES
    step = b * ngrp + g
    last = nseq * ngrp - 1
    nslot = kbuf.shape[0]
    ahead = nslot - 1
    slot = step % nslot

    def issue(bb, gg, sl, pg):
        page = pt_ref[bb * n_pages + pg]
        _pages_copy(ck_ref, layer, page, gg * hg, kbuf, sl, pg, sem.at[0, sl]).start()
        _pages_copy(cv_ref, layer, page, gg * hg, vbuf, sl, pg, sem.at[1, sl]).start()

    def wait_all(sl):
        for pg in range(n_pages):
            _pages_copy(ck_ref, layer, 0, 0, kbuf, sl, pg, sem.at[0, sl]).wait()
        for pg in range(n_pages):
            _pages_copy(cv_ref, layer, 0, 0, vbuf, sl, pg, sem.at[1, sl]).wait()

    @pl.when(step == 0)
    def _():
        for d in range(ahead):
            first = jnp.minimum(d, last)
            for pg in range(n_pages):
                issue(first // ngrp, first % ngrp, d, pg)

    wait_all(slot)
    nxt = jnp.minimum(step + ahead, last)
    nxt_b, nxt_g = nxt // ngrp, nxt % ngrp
    nxt_slot = (step + ahead) % nslot

    ts = q_ref.shape[2]
    lane = lax.broadcasted_iota(jnp.int32, (ts, LANES), 1)
    trow = lax.broadcasted_iota(jnp.int32, (ts, LANES), 0)
    mlane = lax.broadcasted_iota(jnp.int32, (hd, LANES), 1)
    pad = jnp.zeros((LANES - ts, hd), jnp.float32)
    heads = range(hg)
    mts = []
    for h in heads:
        mt = jnp.zeros((hd, LANES), jnp.float32)
        for j in range(nblk):
            x = kbuf[slot, h, :, j * MOBA_BLOCK:j * MOBA_BLOCK + LANES]
            for pp in range(1, ppb):
                x = x + kbuf[slot, h, :, j * MOBA_BLOCK + pp * LANES:j * MOBA_BLOCK + (pp + 1) * LANES]
            mt = jnp.where(mlane == j, jnp.sum(x, axis=1, keepdims=True), mt)
            if h == 0:
                for pp in range(ppb):
                    issue(nxt_b, nxt_g, nxt_slot, j * ppb + pp)
        mts.append(mt * (1.0 / MOBA_BLOCK))
    qs = [q_ref[0, h] for h in heads]
    sbs = [jnp.where(lane < nblk,
                     jnp.dot(qs[h], mts[h], precision=_HI, preferred_element_type=jnp.float32), NEG)
           for h in heads]
    qbs = [_bf(q * (hd ** -0.5)) for q in qs]
    for h in heads:
        s_ref[h] = jnp.dot(qbs[h], _bf(kbuf[slot, h]), preferred_element_type=jnp.float32)
    sels = []
    for h in heads:
        rank = jnp.zeros((ts, LANES), jnp.float32)
        for jp in range(nblk):
            col = sbs[h][:, jp:jp + 1]
            before = (col > sbs[h]) | ((col == sbs[h]) & (lane > jp))
            rank = rank + jnp.where(before, 1.0, 0.0)
        sels.append(jnp.where((rank < MOBA_TOPK) & (lane < nblk), 1.0, 0.0))
    vns, pos, ls = [], [], []
    for h in heads:
        s = jnp.concatenate(
            [jnp.where(sels[h][:, j:j + 1] > 0.0, s_ref[h, :, j * MOBA_BLOCK:(j + 1) * MOBA_BLOCK],
                       NEG) for j in range(nblk)], axis=1)
        kn = _bf(jnp.concatenate([kn_ref[0, h], pad], axis=0))
        vns.append(_bf(jnp.concatenate([vn_ref[0, h], pad], axis=0)))
        so = lax.dot_general(qbs[h], kn, _NT, preferred_element_type=jnp.float32)
        so = jnp.where(lane <= trow, so, NEG)
        m = jnp.maximum(jnp.max(s, axis=1, keepdims=True), jnp.max(so, axis=1, keepdims=True))
        p = jnp.exp(s - m)
        po = jnp.exp(so - m)
        ls.append(jnp.sum(p, axis=1, keepdims=True) + jnp.sum(po, axis=1, keepdims=True))
        p_ref[h] = _bf(p)
        pos.append(_bf(po))
    for h in heads:
        o = (lax.dot_general(p_ref[h], _bf(vbuf[slot, h]), _NT, preferred_element_type=jnp.float32)
             + jnp.dot(pos[h], vns[h], preferred_element_type=jnp.float32))
        o_ref[0, h] = o / ls[h]

    @pl.when(step == last)
    def _():
        for d in range(1, nslot):
            wait_all((step + d) % nslot)


def _moba_sample(layer, page_table, q_hm, k_hm, v_hm, cache_kt, cache_vt):
    db, nh, ts, hd = q_hm.shape
    n_pages = page_table.shape[1]
    page = cache_kt.shape[4]
    hg = SAMPLE_HEAD_GROUP
    assert page == LANES and MOBA_BLOCK % page == 0 and nh % hg == 0
    assert (n_pages * page) % MOBA_BLOCK == 0 and n_pages * page // MOBA_BLOCK <= LANES
    blk = pl.BlockSpec((1, hg, ts, hd), lambda b, g, pt: (b, g, 0, 0))
    grid_spec = pltpu.PrefetchScalarGridSpec(
        num_scalar_prefetch=1,
        grid=(db, nh // hg),
        in_specs=[blk, blk, blk, pl.BlockSpec(memory_space=pl.ANY), pl.BlockSpec(memory_space=pl.ANY)],
        out_specs=blk,
        scratch_shapes=[pltpu.VMEM((SAMPLE_SLOTS, hg, hd, n_pages * page), jnp.float32),
                        pltpu.VMEM((SAMPLE_SLOTS, hg, hd, n_pages * page), jnp.float32),
                        pltpu.SemaphoreType.DMA((2, SAMPLE_SLOTS)),
                        pltpu.VMEM((hg, ts, n_pages * page), jnp.float32),
                        pltpu.VMEM((hg, ts, n_pages * page), jnp.bfloat16)])
    return pl.pallas_call(
        functools.partial(_moba_sample_kernel, layer, db, n_pages),
        grid_spec=grid_spec,
        out_shape=jax.ShapeDtypeStruct((db, nh, ts, hd), jnp.float32),
        compiler_params=pltpu.CompilerParams(dimension_semantics=("arbitrary", "arbitrary"),
                                             vmem_limit_bytes=VMEM_LIMIT),
        name="moba_sample",
    )(page_table.reshape(-1), q_hm, k_hm, v_hm, cache_kt, cache_vt)


def _gla_chunks(c, qs, ks, gs, vs, sts, seg, ltri, hmask, trow):
    dk, dv = DK_B, DV_B
    n = len(qs)
    heads = range(N_HEADS_B)
    f32 = jnp.float32
    zero = jnp.zeros((c, D_BK), f32)
    cums = [jnp.dot(ltri, g, precision=_HI, preferred_element_type=f32) for g in gs]
    pmats = []
    for q, k, cum in zip(qs, ks, cums):
        parts = []
        for s in range(c):
            e = jnp.exp(jnp.where(trow >= s, cum - cum[s:s + 1, :], NEG))
            parts.append(q * k[s:s + 1, :] * e)
        pmats.append(_bf(jnp.concatenate(parts, axis=0)))
    rs = [jnp.dot(pm, seg, preferred_element_type=f32) for pm in pmats]
    by_head = lambda x: _bf(jnp.concatenate([jnp.where(hmask[h], x, zero) for h in heads], axis=0))
    qes = [by_head(q * jnp.exp(cum)) for q, cum in zip(qs, cums)]
    kds = [by_head(k * jnp.exp(cum[c - 1:c, :] - cum)) for k, cum in zip(ks, cums)]
    inters = [lax.dot_general(qes[i], _bf(sts[i]), _NT, preferred_element_type=f32)
              for i in range(n)]
    vstk = [_bf(jnp.concatenate([v[:, h * dv:(h + 1) * dv] for h in heads], axis=0)) for v in vs]
    upds = [lax.dot_general(vstk[i], kds[i], _TN, preferred_element_type=f32) for i in range(n)]
    outs, new = [], []
    for i in range(n):
        r, v = rs[i], vs[i]
        o = r[0:c] * v[0:1, :]
        for s in range(1, c):
            o = o + r[s * c:(s + 1) * c] * v[s:s + 1, :]
        outs.append(o + jnp.concatenate([inters[i][h * c:(h + 1) * c] for h in heads], axis=1))
        new.append(sts[i] * jnp.exp(cums[i][c - 1:c, :]) + upds[i])
    return outs, new


def _gla_kernel(chunk, q_ref, k_ref, g_ref, v_ref, s0_ref, seg_ref, o_ref, sout_ref, s_ref):
    tb = pl.program_id(1)
    c = chunk
    nsb = q_ref.shape[0]
    nchunk = q_ref.shape[1] // c

    @pl.when(tb == 0)
    def _():
        for sq in range(nsb):
            s_ref[sq] = s0_ref[sq].T

    rows = lax.broadcasted_iota(jnp.int32, (c, c), 0)
    cols = lax.broadcasted_iota(jnp.int32, (c, c), 1)
    ltri = jnp.where(rows >= cols, 1.0, 0.0)
    trow = lax.broadcasted_iota(jnp.int32, (c, D_BK), 0)
    lane = lax.broadcasted_iota(jnp.int32, (c, D_BK), 1)
    hmask = [(lane >= h * DK_B) & (lane < (h + 1) * DK_B) for h in range(N_HEADS_B)]

    def body(ci, carry):
        t0 = pl.multiple_of(ci * c, c)
        tok = lambda ref: [ref[sq, pl.ds(t0, c), :] for sq in range(nsb)]
        outs, new = _gla_chunks(c, tok(q_ref), tok(k_ref), tok(g_ref), tok(v_ref),
                                [s_ref[sq] for sq in range(nsb)], seg_ref[...], ltri, hmask, trow)
        for sq in range(nsb):
            o_ref[sq, pl.ds(t0, c), :] = outs[sq]
            s_ref[sq] = new[sq]
        return carry

    lax.fori_loop(0, nchunk, body, 0, unroll=4 if nchunk % 4 == 0 else 1)

    @pl.when(tb == pl.num_programs(1) - 1)
    def _():
        for sq in range(nsb):
            sout_ref[sq] = s_ref[sq].T


def _gla(qb, kb, gk, vb, s0, seg, *, chunk, tblock, nsb):
    nseq, t, _ = qb.shape
    assert t % tblock == 0 and tblock % chunk == 0 and nseq % nsb == 0
    ntb = t // tblock
    tok = lambda w: pl.BlockSpec((nsb, tblock, w), lambda s, tb: (s, tb, 0))
    st_spec = pl.BlockSpec((nsb, D_BK, DV_B), lambda s, tb: (s, 0, 0))
    return pl.pallas_call(
        functools.partial(_gla_kernel, chunk),
        grid=(nseq // nsb, ntb),
        in_specs=[tok(D_BK), tok(D_BK), tok(D_BK), tok(D_BV), st_spec,
                  pl.BlockSpec(seg.shape, lambda s, tb: (0, 0))],
        out_specs=[tok(D_BV), st_spec],
        out_shape=[jax.ShapeDtypeStruct((nseq, t, D_BV), jnp.float32),
                   jax.ShapeDtypeStruct((nseq, D_BK, DV_B), jnp.float32)],
        scratch_shapes=[pltpu.VMEM((nsb, DV_B, D_BK), jnp.float32)],
        compiler_params=pltpu.CompilerParams(dimension_semantics=("arbitrary", "arbitrary"),
                                             vmem_limit_bytes=VMEM_LIMIT),
        name="gla_c%d" % chunk,
    )(qb, kb, gk, vb, s0, seg)


def _post_kernel(x_ref, oa_ref, ob_ref, gb_ref, gn_ref, wo_ref, nm_ref, wu_ref, wd_ref, nf_ref,
                 y_ref, x1_ref, h2_ref, acc_ref):
    kf = pl.program_id(1)

    @pl.when(kf == 0)
    def _():
        ob = ob_ref[...]
        gb = gb_ref[...]
        gn = gn_ref[...]
        heads = []
        for h in range(N_HEADS_B):
            sl = slice(h * DV_B, (h + 1) * DV_B)
            gate = gb[:, sl]
            heads.append(_rms(ob[:, sl], gn[:, sl]) * (gate * (1.0 / (1.0 + jnp.exp(-gate)))))
        mix = _bf(jnp.concatenate([oa_ref[...]] + heads, axis=1))
        x1 = x_ref[...] + jnp.dot(mix, wo_ref[...], preferred_element_type=jnp.float32)
        x1_ref[...] = x1
        h2_ref[...] = _bf(_rms(x1, nm_ref[...]))
        acc_ref[...] = jnp.zeros_like(acc_ref)

    u = jnp.maximum(jnp.dot(h2_ref[...], wu_ref[...], preferred_element_type=jnp.float32), 0.0)
    acc_ref[...] += jnp.dot(_bf(u * u), wd_ref[...], preferred_element_type=jnp.float32)

    @pl.when(kf == pl.num_programs(1) - 1)
    def _():
        y_ref[...] = _rms(x1_ref[...] + acc_ref[...], nf_ref[...])


def _post(x2d, oa, ob, gb, gla_norm, w_o, norm_mlp, w_up, w_down, norm_final):
    n = x2d.shape[0]
    tm = POST_TILE if n % POST_TILE == 0 else TOKEN_TILE
    tf = FF_TILE
    assert n % tm == 0 and D_FF % tf == 0
    row = lambda w: pl.BlockSpec((tm, w), lambda i, kf: (i, 0))
    full = lambda a: pl.BlockSpec(a.shape, lambda i, kf: (0,) * a.ndim)
    return pl.pallas_call(
        _post_kernel,
        grid=(n // tm, D_FF // tf),
        in_specs=[row(D_MODEL), row(D_A), row(D_BV), row(D_BV), full(gla_norm), full(w_o),
                  full(norm_mlp),
                  pl.BlockSpec((D_MODEL, tf), lambda i, kf: (0, kf)),
                  pl.BlockSpec((tf, D_MODEL), lambda i, kf: (kf, 0)),
                  full(norm_final)],
        out_specs=row(D_MODEL),
        out_shape=jax.ShapeDtypeStruct((n, D_MODEL), jnp.float32),
        scratch_shapes=[pltpu.VMEM((tm, D_MODEL), jnp.float32),
                        pltpu.VMEM((tm, D_MODEL), jnp.bfloat16),
                        pltpu.VMEM((tm, D_MODEL), jnp.float32)],
        compiler_params=pltpu.CompilerParams(dimension_semantics=("arbitrary", "arbitrary"),
                                             vmem_limit_bytes=VMEM_LIMIT),
        name="post",
    )(x2d, oa, ob, gb, gla_norm, w_o, norm_mlp, w_up, w_down, norm_final)


def _seg_matrix():
    d = np.arange(D_BK)[:, None] // DK_B
    e = np.arange(D_BV)[None, :] // DV_B
    return jnp.asarray(d == e, jnp.bfloat16)


def _layer(l, x_prompt, x_sample, cache_k, cache_v, state_gla, page_table, norm_mix, w_in, w_gk2,
           b_gk, gla_norm, w_o, norm_mlp, w_up, w_down):
    b, t_p, _ = x_prompt.shape
    db, t_s, _ = x_sample.shape
    page = cache_k.shape[3]
    n_pages = page_table.shape[1]
    past = n_pages * page
    assert past % MOBA_BLOCK == 0 and past // MOBA_BLOCK >= MOBA_TOPK
    assert t_s <= 8 and TOKEN_TILE % t_s == 0

    w_in_p = _bf(jnp.concatenate(
        [w_in[l], jnp.zeros((D_MODEL, D_IN_PAD - D_IN), w_in.dtype)], axis=1))
    w_gk2_p = _bf(jnp.concatenate(
        [w_gk2[l], jnp.zeros((LANES - GATE_RANK, D_BK), w_gk2.dtype)], axis=0))
    nmix = norm_mix[l].reshape(1, D_MODEL)
    bg = b_gk[l].reshape(1, D_BK)
    gn = gla_norm[l].reshape(1, D_BV)
    nmlp = norm_mlp[l].reshape(1, D_MODEL)
    wo, wu, wd = _bf(w_o[l]), _bf(w_up[l]), _bf(w_down[l])
    seg = _seg_matrix()

    xp2 = x_prompt.reshape(b * t_p, D_MODEL)
    tab_p = _rope_tables(jnp.arange(t_p, dtype=jnp.int32))
    (qs, qf, kbf, vtb, kt_p, vt_p, means, qb, kb, vb, gb, gk) = _in_proj(
        xp2, nmix, w_in_p, w_gk2_p, bg, tab_p, sample=False, nseq=b, seq_len=t_p)
    nb = t_p // MOBA_BLOCK
    oa_p = _moba_prompt(qs.reshape(b, t_p, D_A), qf.reshape(b, t_p, D_A), kbf.reshape(b, t_p, D_A),
                        vtb, means.reshape(b, nb, D_A))
    s0_p = jnp.zeros((b, D_BK, DV_B), state_gla.dtype)
    ob_p, s_p = _gla(qb.reshape(b, t_p, D_BK), kb.reshape(b, t_p, D_BK), gk.reshape(b, t_p, D_BK),
                     vb.reshape(b, t_p, D_BV), s0_p, seg, chunk=GLA_CHUNK_PROMPT,
                     tblock=min(GLA_TBLOCK, t_p), nsb=b)
    prompt = (xp2, oa_p.reshape(b * t_p, D_A), ob_p.reshape(b * t_p, D_BV), gb)
    to_rows = lambda a: jnp.swapaxes(a.reshape(b, N_HEADS_A, HEAD_DIM_A, t_p), 2, 3)
    k_p, v_p = to_rows(kt_p), to_rows(vt_p)

    xs2 = x_sample.reshape(db * t_s, D_MODEL)
    pos_s = past + jnp.tile(jnp.arange(t_s, dtype=jnp.int32), db)
    tab_s = _rope_tables(pos_s)
    (q_s, k_s, v_s, qb_s, kb_s, vb_s, gb_s, gk_s) = _in_proj(
        xs2, nmix, w_in_p, w_gk2_p, bg, tab_s, sample=True, nseq=db, seq_len=t_s)
    oa_s = _moba_sample(l, page_table, q_s, k_s, v_s,
                        jnp.swapaxes(cache_k, 3, 4), jnp.swapaxes(cache_v, 3, 4))
    oa_s = oa_s.transpose(0, 2, 1, 3).reshape(db * t_s, D_A)
    nsb = GLA_SEQS_PER_STEP if db % GLA_SEQS_PER_STEP == 0 else 1
    ob_s, s_s = _gla(qb_s.reshape(db, t_s, D_BK), kb_s.reshape(db, t_s, D_BK),
                     gk_s.reshape(db, t_s, D_BK), vb_s.reshape(db, t_s, D_BV),
                     state_gla[l].reshape(db, D_BK, DV_B), seg, chunk=t_s, tblock=t_s, nsb=nsb)
    sample = (xs2, oa_s, ob_s.reshape(db * t_s, D_BV), gb_s)
    return prompt, sample, (gn, wo, nmlp, wu, wd), (k_p, v_p, s_p, k_s, v_s, s_s)


def kernel(x_prompt, x_sample, cache_k, cache_v, state_gla, page_table, norm_mix, w_in, w_gk2, b_gk,
           gla_norm, w_o, norm_mlp, w_up, w_down, norm_final):
    depth = norm_mix.shape[0]
    assert depth == 1, "the trunk is one layer deep"
    b, t_p, _ = x_prompt.shape
    db, t_s, _ = x_sample.shape
    prompt, sample, weights, new = _layer(0, x_prompt, x_sample, cache_k, cache_v, state_gla,
                                          page_table, norm_mix, w_in, w_gk2, b_gk, gla_norm, w_o,
                                          norm_mlp, w_up, w_down)
    nf = norm_final.reshape(1, D_MODEL)
    y_p = _post(*prompt, *weights, nf).reshape(b, t_p, D_MODEL)
    y_s = _post(*sample, *weights, nf).reshape(db, t_s, D_MODEL)
    k_p, v_p, s_p, k_s, v_s, s_s = new
    st = lambda s: s.reshape(1, s.shape[0], N_HEADS_B, DK_B, DV_B)
    return (y_p, y_s, k_p[None], v_p[None], st(s_p), k_s[None], v_s[None], st(s_s))
```
